```python
import math
import jax, jax.numpy as jnp
from jax import lax
import numpy as np

D_MODEL = 4096
BATCH = 4
SEQ = 4096
DEPTH = 1

CHUNK = 64
Q_BLOCK = 128
A_HEADS = 16
A_HEAD_DIM = 128
A_KV_DIM = 128
IDX_HEADS = 32
IDX_DIM = 128
TOPK_MAX = 256
B_HEADS = 16
Q_LORA = 1024
KV_LORA = 512
QK_NOPE = 128
QK_ROPE = 64
V_DIM = 128
ROPE_THETA = 10000.0
D_FF = 4 * D_MODEL
EPS = 1e-6

A_WIDTH = A_HEADS * A_HEAD_DIM
B_WIDTH = B_HEADS * V_DIM
MIX_WIDTH = A_WIDTH + B_WIDTH
IN_SPLITS = (A_WIDTH, A_KV_DIM, A_KV_DIM, IDX_HEADS * IDX_DIM, IDX_DIM, IDX_HEADS, Q_LORA, KV_LORA, QK_ROPE)
IN_WIDTH = A_WIDTH + 2 * A_KV_DIM + IDX_HEADS * IDX_DIM + IDX_DIM + IDX_HEADS + Q_LORA + KV_LORA + QK_ROPE

kernel_name = 'hybrid_dsa_mla_sqrelu_adaln_block'


def rmsnorm(x, g):
    xf = x.astype(jnp.float32)
    y = xf * lax.rsqrt(jnp.mean(xf * xf, axis=-1, keepdims=True) + EPS)
    return (y * g.astype(jnp.float32)).astype(x.dtype)


def alibi_slopes():
    return 2.0 ** (-8.0 * jnp.arange(1, A_HEADS + 1, dtype=jnp.float32) / A_HEADS)


def rope_tables(pos):
    inv = ROPE_THETA ** (-jnp.arange(0, QK_ROPE, 2, dtype=jnp.float32) / QK_ROPE)
    ang = pos.astype(jnp.float32)[..., None] * inv
    return jnp.cos(ang), jnp.sin(ang)


def apply_rope(x, cos, sin):
    x1, x2 = jnp.split(x.astype(jnp.float32), 2, axis=-1)
    out = jnp.concatenate([x1 * cos - x2 * sin, x1 * sin + x2 * cos], axis=-1)
    return out.astype(x.dtype)


def to_blocks(a):
    b, s = a.shape[0], a.shape[1]
    a = a.reshape((b, s // Q_BLOCK, Q_BLOCK) + a.shape[2:])
    return jnp.moveaxis(a, 1, 0)


def from_blocks(a):
    a = jnp.moveaxis(a, 0, 1)
    return a.reshape((a.shape[0], a.shape[1] * a.shape[2]) + a.shape[3:])


def sparse_indexer_attention(q, k, v, q_idx, k_idx, w_idx, pos):
    b, s = pos.shape
    topk = min(TOPK_MAX, s // 4)
    chunk_k = pos // CHUNK
    slopes = alibi_slopes()

    def block(xs):
        qb, qib, wb, pb = xs
        admissible = chunk_k[:, None, :] <= (pb // CHUNK)[:, :, None]
        logits = jnp.einsum('bqhd,bsd->bqhs', qib, k_idx).astype(jnp.float32) * (IDX_DIM ** -0.5)
        wts = wb.astype(jnp.float32) * (IDX_HEADS ** -0.5)
        score = jnp.einsum('bqh,bqhs->bqs', wts, jax.nn.relu(logits))
        score = jnp.where(admissible, score, -jnp.inf)
        vals, idx = lax.top_k(score, topk)
        valid = jnp.isfinite(vals)
        flat = idx.reshape(b, -1)
        k_sel = jnp.take_along_axis(k, flat[..., None], axis=1).reshape(b, Q_BLOCK, topk, A_KV_DIM)
        v_sel = jnp.take_along_axis(v, flat[..., None], axis=1).reshape(b, Q_BLOCK, topk, A_KV_DIM)
        p_sel = jnp.take_along_axis(pos, flat, axis=1).reshape(b, Q_BLOCK, topk)
        dist = jnp.abs(pb[:, :, None] - p_sel).astype(jnp.float32)
        sc = jnp.einsum('bqhd,bqkd->bhqk', qb, k_sel).astype(jnp.float32) * (A_HEAD_DIM ** -0.5)
        sc = sc - slopes[None, :, None, None] * dist[:, None]
        sc = jnp.where(valid[:, None], sc, -jnp.inf)
        p = jax.nn.softmax(sc, axis=-1).astype(v.dtype)
        return jnp.einsum('bhqk,bqkd->bqhd', p, v_sel)

    out = lax.map(block, (to_blocks(q), to_blocks(q_idx), to_blocks(w_idx), to_blocks(pos)))
    return from_blocks(out)


def mla_attention(q_nope, q_rope, k_nope, k_rope, v, pos):
    chunk_k = pos // CHUNK
    scale = (QK_NOPE + QK_ROPE) ** -0.5

    def block(xs):
        qn, qr, pb = xs
        admissible = chunk_k[:, None, :] <= (pb // CHUNK)[:, :, None]
        sc = jnp.einsum('bqhd,bshd->bhqs', qn, k_nope) + jnp.einsum('bqhr,bsr->bhqs', qr, k_rope)
        sc = jnp.where(admissible[:, None], sc.astype(jnp.float32) * scale, -jnp.inf)
        p = jax.nn.softmax(sc, axis=-1).astype(v.dtype)
        return jnp.einsum('bhqs,bshd->bqhd', p, v)

    out = lax.map(block, (to_blocks(q_nope), to_blocks(q_rope), to_blocks(pos)))
    return from_blocks(out)


def setup_inputs(seed: int = 0) -> dict:
    key = jax.random.key(seed)
    ks = jax.random.split(key, 20)
    f32 = jnp.float32

    def nrm(k, shape, scale):
        return jax.random.normal(k, shape, f32) * scale

    def gain(k, shape):
        return 1.0 + 0.1 * jax.random.normal(k, shape, f32)

    offsets = jax.random.randint(ks[2], (BATCH,), 0, 16) * CHUNK
    positions = (offsets[:, None] + jnp.arange(SEQ, dtype=jnp.int32)[None, :]).astype(jnp.int32)
    return {
        'x': nrm(ks[0], (BATCH, SEQ, D_MODEL), 1.0),
        'c': nrm(ks[1], (BATCH, D_MODEL), 1.0),
        'positions': positions,
        'w_ada': nrm(ks[3], (DEPTH, D_MODEL, 6 * D_MODEL), 0.3 * D_MODEL ** -0.5),
        'b_ada': nrm(ks[4], (DEPTH, 6 * D_MODEL), 0.1),
        'ln1_g': gain(ks[5], (DEPTH, D_MODEL)),
        'w_in': nrm(ks[6], (DEPTH, D_MODEL, IN_WIDTH), D_MODEL ** -0.5),
        'q_norm_g': gain(ks[7], (DEPTH, Q_LORA)),
        'kv_norm_g': gain(ks[8], (DEPTH, KV_LORA)),
        'w_uq': nrm(ks[9], (DEPTH, Q_LORA, B_HEADS * (QK_NOPE + QK_ROPE)), Q_LORA ** -0.5),
        'w_uk': nrm(ks[10], (DEPTH, KV_LORA, B_HEADS * QK_NOPE), KV_LORA ** -0.5),
        'w_uv': nrm(ks[11], (DEPTH, KV_LORA, B_HEADS * V_DIM), KV_LORA ** -0.5),
        'w_o': nrm(ks[12], (DEPTH, MIX_WIDTH, D_MODEL), MIX_WIDTH ** -0.5),
        'ln2_g': gain(ks[13], (DEPTH, D_MODEL)),
        'w_mlp_in': nrm(ks[14], (DEPTH, D_MODEL, D_FF), D_MODEL ** -0.5),
        'w_mlp_out': nrm(ks[15], (DEPTH, D_FF, D_MODEL), D_FF ** -0.5),
        'final_g': gain(ks[16], (D_MODEL,)),
    }


def reference(x, c, positions, w_ada, b_ada, ln1_g, w_in, q_norm_g, kv_norm_g, w_uq, w_uk, w_uv,
              w_o, ln2_g, w_mlp_in, w_mlp_out, final_g):
    b, s, _ = x.shape
    split_points = np.cumsum(IN_SPLITS)[:-1].tolist()
    cos, sin = rope_tables(positions)
    for l in range(DEPTH):
        mod = jnp.einsum('bd,de->be', jax.nn.silu(c), w_ada[l]) + b_ada[l]
        shift1, scale1, gate1, shift2, scale2, gate2 = jnp.split(mod, 6, axis=-1)

        h = rmsnorm(x, ln1_g[l]) * (1.0 + scale1[:, None]) + shift1[:, None]
        proj = jnp.einsum('bsd,de->bse', h, w_in[l])
        q_a, k_a, v_a, q_i, k_i, w_i, c_q, c_kv, k_r = jnp.split(proj, split_points, axis=-1)

        out_a = sparse_indexer_attention(
            q_a.reshape(b, s, A_HEADS, A_HEAD_DIM), k_a, v_a,
            q_i.reshape(b, s, IDX_HEADS, IDX_DIM), k_i, w_i, positions)

        q_full = jnp.einsum('bsr,re->bse', rmsnorm(c_q, q_norm_g[l]), w_uq[l])
        q_full = q_full.reshape(b, s, B_HEADS, QK_NOPE + QK_ROPE)
        q_nope, q_rope = q_full[..., :QK_NOPE], q_full[..., QK_NOPE:]
        q_rope = apply_rope(q_rope, cos[:, :, None], sin[:, :, None])
        kv_lat = rmsnorm(c_kv, kv_norm_g[l])
        k_nope = jnp.einsum('bsr,re->bse', kv_lat, w_uk[l]).reshape(b, s, B_HEADS, QK_NOPE)
        v_b = jnp.einsum('bsr,re->bse', kv_lat, w_uv[l]).reshape(b, s, B_HEADS, V_DIM)
        k_rope = apply_rope(k_r, cos, sin)
        out_b = mla_attention(q_nope, q_rope, k_nope, k_rope, v_b, positions)

        mix = jnp.concatenate([out_a.reshape(b, s, A_WIDTH), out_b.reshape(b, s, B_WIDTH)], axis=-1)
        x = x + gate1[:, None] * jnp.einsum('bse,ed->bsd', mix, w_o[l])

        h2 = rmsnorm(x, ln2_g[l]) * (1.0 + scale2[:, None]) + shift2[:, None]
        hid = jnp.square(jax.nn.relu(jnp.einsum('bsd,df->bsf', h2, w_mlp_in[l])))
        x = x + gate2[:, None] * jnp.einsum('bsf,fd->bsd', hid, w_mlp_out[l])
    return rmsnorm(x, final_g)
```

```python
import functools
import math

import jax
import jax.numpy as jnp
from jax import lax
from jax.experimental import pallas as pl
from jax.experimental.pallas import tpu as pltpu

F32 = jnp.float32
BF16 = jnp.bfloat16

EPS = 1e-6
CHUNK = 64
CHUNK_SHIFT = 6
A_HEADS = 16
IDX_HEADS = 32
HEAD_DIM = 128
TOPK_MAX = 256
B_HEADS = 16
QK_ROPE = 64
ROPE_THETA = 10000.0

LANES = 128
MASKED = -1e30
INT_MIN = -(2 ** 31)
NEG_INF_KEY = -2139095041
VMEM_LIMIT = 56 * 1024 * 1024


def _params(semantics, vmem=VMEM_LIMIT):
    return pltpu.CompilerParams(dimension_semantics=semantics, vmem_limit_bytes=vmem)


def _dot_nt(a, b):
    return lax.dot_general(a, b, (((1,), (1,)), ((), ())), preferred_element_type=F32)


def _dot_tn(a, b):
    return lax.dot_general(a, b, (((0,), (0,)), ((), ())), preferred_element_type=F32)


def _mod_kernel(c_ref, w_ref, b_ref, o_ref):
    c = c_ref[...]
    s = c * (1.0 / (1.0 + jnp.exp(-c)))
    o_ref[...] = jnp.dot(s.astype(BF16), w_ref[...].astype(BF16),
                         preferred_element_type=F32) + b_ref[...]


def _adaln_mod(c, w_ada, b_ada, tn):
    b, d = c.shape
    n = w_ada.shape[1]
    rows = 8
    c_pad = jnp.zeros((rows, d), F32).at[:b].set(c)
    out = pl.pallas_call(
        _mod_kernel,
        grid=(n // tn,),
        in_specs=[pl.BlockSpec((rows, d), lambda j: (0, 0)),
                  pl.BlockSpec((d, tn), lambda j: (0, j)),
                  pl.BlockSpec((1, tn), lambda j: (0, j))],
        out_specs=pl.BlockSpec((rows, tn), lambda j: (0, j)),
        out_shape=jax.ShapeDtypeStruct((rows, n), F32),
        compiler_params=_params(("arbitrary",)),
        name="adaln_mod",
    )(c_pad, w_ada, b_ada.reshape(1, n))
    return out[:b]


def _ln_mod_kernel(x_ref, g_ref, sc_ref, sh_ref, o_ref):
    x = x_ref[...]
    y = x * lax.rsqrt(jnp.mean(x * x, axis=-1, keepdims=True) + EPS) * g_ref[...]
    o_ref[...] = (y * (1.0 + sc_ref[0]) + sh_ref[0]).astype(o_ref.dtype)


def _ln_kernel(x_ref, g_ref, o_ref):
    x = x_ref[...]
    o_ref[...] = (x * lax.rsqrt(jnp.mean(x * x, axis=-1, keepdims=True) + EPS)
                  * g_ref[...]).astype(o_ref.dtype)


def _ln_mod(x2d, g, scale, shift, seq, tm):
    t, d = x2d.shape
    nb = scale.shape[0]
    per = seq // tm
    return pl.pallas_call(
        _ln_mod_kernel,
        grid=(t // tm,),
        in_specs=[pl.BlockSpec((tm, d), lambda i: (i, 0)),
                  pl.BlockSpec((1, d), lambda i: (0, 0)),
                  pl.BlockSpec((1, 1, d), lambda i: (i // per, 0, 0)),
                  pl.BlockSpec((1, 1, d), lambda i: (i // per, 0, 0))],
        out_specs=pl.BlockSpec((tm, d), lambda i: (i, 0)),
        out_shape=jax.ShapeDtypeStruct((t, d), BF16),
        compiler_params=_params(("parallel",)),
        name="ln_mod",
    )(x2d, g.reshape(1, d), scale.reshape(nb, 1, d), shift.reshape(nb, 1, d))


def _ln_final(x2d, g, tm):
    t, d = x2d.shape
    return pl.pallas_call(
        _ln_kernel,
        grid=(t // tm,),
        in_specs=[pl.BlockSpec((tm, d), lambda i: (i, 0)),
                  pl.BlockSpec((1, d), lambda i: (0, 0))],
        out_specs=pl.BlockSpec((tm, d), lambda i: (i, 0)),
        out_shape=jax.ShapeDtypeStruct((t, d), F32),
        compiler_params=_params(("parallel",)),
        name="ln_final",
    )(x2d, g.reshape(1, d))


def _inproj_kernel(h_ref, w_ref, o1_ref, o2_ref, *, n1):
    j = pl.program_id(1)
    acc = jnp.dot(h_ref[...], w_ref[...], preferred_element_type=F32)

    @pl.when(j < n1)
    def _():
        o1_ref[...] = acc.astype(o1_ref.dtype)

    @pl.when(j >= n1)
    def _():
        o2_ref[...] = acc


def _inproj(h, w, width1, tm, tn):
    t, d = h.shape
    n = w.shape[1]
    n1 = width1 // tn
    return pl.pallas_call(
        functools.partial(_inproj_kernel, n1=n1),
        grid=(t // tm, n // tn),
        in_specs=[pl.BlockSpec((tm, d), lambda i, j: (i, 0)),
                  pl.BlockSpec((d, tn), lambda i, j: (0, j))],
        out_specs=[pl.BlockSpec((tm, tn), lambda i, j: (i, jnp.minimum(j, n1 - 1))),
                   pl.BlockSpec((tm, tn), lambda i, j: (i, jnp.maximum(j - n1, 0)))],
        out_shape=[jax.ShapeDtypeStruct((t, width1), BF16),
                   jax.ShapeDtypeStruct((t, n - width1), F32)],
        compiler_params=_params(("parallel", "arbitrary")),
        name="in_proj",
    )(h, w)


def _mm_relu2_kernel(a_ref, w_ref, o_ref):
    acc = jnp.dot(a_ref[...], w_ref[...], preferred_element_type=F32)
    r = jnp.maximum(acc, 0.0)
    o_ref[...] = (r * r).astype(o_ref.dtype)


def _mm_relu2(a, w, tm, tn):
    t, d = a.shape
    n = w.shape[1]
    return pl.pallas_call(
        _mm_relu2_kernel,
        grid=(t // tm, n // tn),
        in_specs=[pl.BlockSpec((tm, d), lambda i, j: (i, 0)),
                  pl.BlockSpec((d, tn), lambda i, j: (0, j))],
        out_specs=pl.BlockSpec((tm, tn), lambda i, j: (i, j)),
        out_shape=jax.ShapeDtypeStruct((t, n), BF16),
        compiler_params=_params(("parallel", "arbitrary")),
        name="mlp_in",
    )(a, w)


def _mm_resid_kernel(a_ref, w_ref, r_ref, g_ref, o_ref, acc_ref, *, nk):
    k = pl.program_id(2)
    part = jnp.dot(a_ref[...], w_ref[...], preferred_element_type=F32)

    @pl.when(k == 0)
    def _():
        acc_ref[...] = part

    @pl.when(k > 0)
    def _():
        acc_ref[...] += part

    @pl.when(k == nk - 1)
    def _():
        o_ref[...] = r_ref[...] + g_ref[0] * acc_ref[...]


def _mm_resid(a, w, resid, gate, seq, tm, tn, tk, name):
    t, kdim = a.shape
    n = w.shape[1]
    nb = gate.shape[0]
    per = seq // tm
    nk = kdim // tk
    return pl.pallas_call(
        functools.partial(_mm_resid_kernel, nk=nk),
        grid=(t // tm, n // tn, nk),
        in_specs=[pl.BlockSpec((tm, tk), lambda i, j, k: (i, k)),
                  pl.BlockSpec((tk, tn), lambda i, j, k: (k, j)),
                  pl.BlockSpec((tm, tn), lambda i, j, k: (i, j)),
                  pl.BlockSpec((1, 1, tn), lambda i, j, k: (i // per, 0, j))],
        out_specs=pl.BlockSpec((tm, tn), lambda i, j, k: (i, j)),
        out_shape=jax.ShapeDtypeStruct((t, n), F32),
        scratch_shapes=[pltpu.VMEM((tm, tn), F32)],
        compiler_params=_params(("parallel", "parallel", "arbitrary")),
        name=name,
    )(a, w, resid, gate.reshape(nb, 1, n))


def _mix_resid_kernel(a1_ref, a2_ref, w1_ref, w2_ref, r_ref, g_ref, o_ref):
    acc = jnp.dot(a1_ref[...], w1_ref[...], preferred_element_type=F32)
    acc += jnp.dot(a2_ref[...], w2_ref[...], preferred_element_type=F32)
    o_ref[...] = r_ref[...] + g_ref[0] * acc


def _mix_resid(a1, a2, w, resid, gate, seq, tm, tn):
    t, k1 = a1.shape
    k2 = a2.shape[1]
    assert k1 == k2
    n = w.shape[1]
    nb = gate.shape[0]
    per = seq // tm
    return pl.pallas_call(
        _mix_resid_kernel,
        grid=(t // tm, n // tn),
        in_specs=[pl.BlockSpec((tm, k1), lambda i, j: (i, 0)),
                  pl.BlockSpec((tm, k2), lambda i, j: (i, 0)),
                  pl.BlockSpec((k1, tn), lambda i, j: (0, j)),
                  pl.BlockSpec((k2, tn), lambda i, j: (1, j)),
                  pl.BlockSpec((tm, tn), lambda i, j: (i, j)),
                  pl.BlockSpec((1, 1, tn), lambda i, j: (i // per, 0, j))],
        out_specs=pl.BlockSpec((tm, tn), lambda i, j: (i, j)),
        out_shape=jax.ShapeDtypeStruct((t, n), F32),
        compiler_params=_params(("parallel", "arbitrary")),
        name="out_proj",
    )(a1, a2, w, w, resid, gate.reshape(nb, 1, n))


def _rope_tables(pos_col, inv_row):
    ang = pos_col.astype(F32) * inv_row
    lane = lax.broadcasted_iota(jnp.int32, ang.shape, 1)
    half = QK_ROPE // 2
    cos = jnp.cos(ang)
    sin = jnp.sin(ang)
    c = jnp.where(lane < QK_ROPE, cos, 0.0)
    s_lo = jnp.where(lane < half, -sin, 0.0)
    s_hi = jnp.where((lane >= half) & (lane < QK_ROPE), sin, 0.0)
    return c, s_lo, s_hi


def _rope_apply(x, c, s_lo, s_hi):
    half = QK_ROPE // 2
    return (x * c + pltpu.roll(x, LANES - half, axis=1) * s_lo
            + pltpu.roll(x, half, axis=1) * s_hi)


def _mla_proj_kernel(cq_ref, ckv_ref, misc_ref, pos_ref, inv_ref, gq_ref, gkv_ref,
                     wqn_ref, wqr_ref, wuk_ref, wuv_ref,
                     qn_ref, qr_ref, kn_ref, vb_ref, kr_ref, *, heads):
    cq = cq_ref[...]
    cqn = (cq * lax.rsqrt(jnp.mean(cq * cq, axis=-1, keepdims=True) + EPS)
           * gq_ref[...]).astype(BF16)
    ckv = ckv_ref[...]
    kvn = (ckv * lax.rsqrt(jnp.mean(ckv * ckv, axis=-1, keepdims=True) + EPS)
           * gkv_ref[...]).astype(BF16)
    c, s_lo, s_hi = _rope_tables(pos_ref[...], inv_ref[...])

    qn_ref[...] = jnp.dot(cqn, wqn_ref[...], preferred_element_type=F32).astype(BF16)
    qr = jnp.dot(cqn, wqr_ref[...], preferred_element_type=F32)
    for h in range(heads):
        sl = slice(h * LANES, (h + 1) * LANES)
        qr_ref[:, sl] = _rope_apply(qr[:, sl], c, s_lo, s_hi).astype(BF16)
    kn_ref[...] = jnp.dot(kvn, wuk_ref[...], preferred_element_type=F32).astype(BF16)
    vb_ref[...] = jnp.dot(kvn, wuv_ref[...], preferred_element_type=F32).astype(BF16)
    kr_ref[...] = _rope_apply(misc_ref[...], c, s_lo, s_hi).astype(BF16)


def _mla_proj(p2, pos_col, inv_row, gq, gkv, wqn, wqr, wuk, wuv, q_lora, kv_lora, heads, tm):
    t = p2.shape[0]
    hw = heads * LANES
    misc_blk = p2.shape[1] // LANES - 1
    const = lambda i: (0, 0)
    return pl.pallas_call(
        functools.partial(_mla_proj_kernel, heads=heads),
        grid=(t // tm,),
        in_specs=[pl.BlockSpec((tm, q_lora), lambda i: (i, 0)),
                  pl.BlockSpec((tm, kv_lora), lambda i: (i, q_lora // kv_lora)),
                  pl.BlockSpec((tm, LANES), lambda i: (i, misc_blk)),
                  pl.BlockSpec((tm, 1), lambda i: (i, 0)),
                  pl.BlockSpec((1, LANES), const),
                  pl.BlockSpec((1, q_lora), const),
                  pl.BlockSpec((1, kv_lora), const),
                  pl.BlockSpec((q_lora, hw), const),
                  pl.BlockSpec((q_lora, hw), const),
                  pl.BlockSpec((kv_lora, hw), const),
                  pl.BlockSpec((kv_lora, hw), const)],
        out_specs=[pl.BlockSpec((tm, hw), lambda i: (i, 0))] * 4
                  + [pl.BlockSpec((tm, LANES), lambda i: (i, 0))],
        out_shape=[jax.ShapeDtypeStruct((t, hw), BF16)] * 4
                  + [jax.ShapeDtypeStruct((t, LANES), BF16)],
        compiler_params=_params(("parallel",)),
        name="mla_proj",
    )(p2, p2, p2, pos_col, inv_row, gq, gkv, wqn, wqr, wuk, wuv)


def _mla_attn_kernel(it_ref, jt_ref, qn_ref, qr_ref, kn_ref, kr_ref, v_ref, qpos_ref, kpos_ref,
                     o_ref, m_ref, l_ref, acc_ref, *, scale, last_of):
    p = pl.program_id(2)
    i = it_ref[p]
    j = jt_ref[p]

    @pl.when(j == 0)
    def _():
        m_ref[...] = jnp.full(m_ref.shape, MASKED, F32)
        l_ref[...] = jnp.zeros(l_ref.shape, F32)
        acc_ref[...] = jnp.zeros(acc_ref.shape, F32)

    q2 = jnp.concatenate([qn_ref[...], qr_ref[...]], axis=1)
    k2 = jnp.concatenate([kn_ref[...], kr_ref[...]], axis=1)
    st = _dot_nt(k2, q2) * scale
    adm = (kpos_ref[...] >> CHUNK_SHIFT) <= (qpos_ref[0] >> CHUNK_SHIFT)
    st = jnp.where(adm, st, MASKED)
    m_old = m_ref[...]
    m_new = jnp.maximum(m_old, jnp.max(st, axis=0, keepdims=True))
    alpha = jnp.exp(m_old - m_new)
    pt = jnp.exp(st - m_new)
    l_ref[...] = alpha * l_ref[...] + jnp.sum(pt, axis=0, keepdims=True)
    m_ref[...] = m_new
    acc_ref[...] = alpha * acc_ref[...] + _dot_tn(v_ref[...], pt.astype(BF16))

    @pl.when(j == last_of(i))
    def _():
        o_ref[...] = (acc_ref[...] / l_ref[...]).T.astype(o_ref.dtype)


def _mla_attention(qn, qr, kn, kr, vb, pos_row, pos_col, batch, seq, heads, tq, tk):
    nq = seq // tq
    nk = seq // tk
    pairs = [(i, j) for i in range(nq) for j in range(nk) if j * tk < (i + 1) * tq]
    it = jnp.asarray([p[0] for p in pairs], jnp.int32)
    jt = jnp.asarray([p[1] for p in pairs], jnp.int32)
    last_of = lambda i: ((i + 1) * tq - 1) // tk
    scale = (HEAD_DIM + QK_ROPE) ** -0.5
    qmap = lambda b, h, p, it, jt: (b * nq + it[p], h)
    kmap = lambda b, h, p, it, jt: (b * nk + jt[p], h)
    grid_spec = pltpu.PrefetchScalarGridSpec(
        num_scalar_prefetch=2,
        grid=(batch, heads, len(pairs)),
        in_specs=[pl.BlockSpec((tq, LANES), qmap),
                  pl.BlockSpec((tq, LANES), qmap),
                  pl.BlockSpec((tk, LANES), kmap),
                  pl.BlockSpec((tk, LANES), lambda b, h, p, it, jt: (b * nk + jt[p], 0)),
                  pl.BlockSpec((tk, LANES), kmap),
                  pl.BlockSpec((1, 1, tq), lambda b, h, p, it, jt: (b, 0, it[p])),
                  pl.BlockSpec((tk, 1), lambda b, h, p, it, jt: (b * nk + jt[p], 0))],
        out_specs=pl.BlockSpec((tq, LANES), qmap),
        scratch_shapes=[pltpu.VMEM((1, tq), F32), pltpu.VMEM((1, tq), F32),
                        pltpu.VMEM((LANES, tq), F32)],
    )
    return pl.pallas_call(
        functools.partial(_mla_attn_kernel, scale=scale, last_of=last_of),
        grid_spec=grid_spec,
        out_shape=jax.ShapeDtypeStruct((batch * seq, heads * LANES), BF16),
        compiler_params=_params(("parallel", "parallel", "arbitrary")),
        name="mla_attn",
    )(it, jt, qn, qr, kn, kr, vb, pos_row, pos_col)


def _dsa_kernel(qi_ref, qa_ref, misc_ref, kidx_ref, ka_ref, va_ref, qpos_ref, kpos_ref, o_ref,
                qis_ref, qas_ref, wt_ref, skey_ref, m_ref, l_ref, acc_ref,
                *, tq, kc, idx_heads, heads, topk, hg):
    i = pl.program_id(1)
    nkc = ((i + 1) * tq + kc - 1) // kc

    for h in range(idx_heads):
        qis_ref[h * tq:(h + 1) * tq, :] = qi_ref[:, h * LANES:(h + 1) * LANES]
    for h in range(heads):
        qas_ref[h * tq:(h + 1) * tq, :] = qa_ref[:, h * LANES:(h + 1) * LANES]
    wt_ref[...] = misc_ref[...].T[QK_ROPE:QK_ROPE + idx_heads, :] * (
        (idx_heads ** -0.5) * (HEAD_DIM ** -0.5))
    qpos = qpos_ref[0]
    qchunk = qpos >> CHUNK_SHIFT

    def score_chunk(c, carry):
        r0 = pl.multiple_of(c * kc, kc)
        kx = kidx_ref[pl.ds(r0, kc), :].astype(BF16)
        score = jnp.zeros((kc, tq), F32)
        for g in range(idx_heads // hg):
            lt = _dot_nt(kx, qis_ref[g * hg * tq:(g + 1) * hg * tq, :])
            for hh in range(hg):
                h = g * hg + hh
                score += jnp.maximum(lt[:, hh * tq:(hh + 1) * tq], 0.0) * wt_ref[h:h + 1, :]
        adm = (kpos_ref[pl.ds(r0, kc), :] >> CHUNK_SHIFT) <= qchunk
        bits = pltpu.bitcast(jnp.where(adm, score, -jnp.inf), jnp.int32)
        skey_ref[pl.ds(r0, kc), :] = bits ^ ((bits >> 31) & 0x7FFFFFFF)
        return carry

    lax.fori_loop(0, nkc, score_chunk, 0)

    def count_ge(thr):
        def body(c, cnt):
            r0 = pl.multiple_of(c * kc, kc)
            hit = jnp.where(skey_ref[pl.ds(r0, kc), :] >= thr, 1.0, 0.0)
            return cnt + jnp.sum(hit.reshape(kc // 8, 8, tq), axis=0)
        cnt8 = lax.fori_loop(0, nkc, body, jnp.zeros((8, tq), F32))
        return jnp.sum(cnt8, axis=0, keepdims=True)

    need = float(topk)
    thr = jnp.where(count_ge(jnp.zeros((1, tq), jnp.int32)) >= need, 0, INT_MIN)

    def bit_step(it, thr):
        cand = thr + lax.shift_left(jnp.int32(1), 30 - it)
        return jnp.where(count_ge(cand) >= need, cand, thr)

    thr = lax.fori_loop(0, 31, bit_step, thr)
    thr = jnp.maximum(thr, NEG_INF_KEY + 1)

    m_ref[...] = jnp.full(m_ref.shape, MASKED, F32)
    l_ref[...] = jnp.zeros(l_ref.shape, F32)
    acc_ref[...] = jnp.zeros(acc_ref.shape, F32)
    scale = HEAD_DIM ** -0.5

    def attend_chunk(c, carry):
        r0 = pl.multiple_of(c * kc, kc)
        kx = ka_ref[pl.ds(r0, kc), :].astype(BF16)
        vx = va_ref[pl.ds(r0, kc), :].astype(BF16)
        sel = skey_ref[pl.ds(r0, kc), :] >= thr
        dist = jnp.abs(kpos_ref[pl.ds(r0, kc), :] - qpos).astype(F32)
        st = _dot_nt(kx, qas_ref[...])
        ps = []
        for h in range(heads):
            sl = slice(h * tq, (h + 1) * tq)
            slope = 2.0 ** (-8.0 * (h + 1) / heads)
            s = jnp.where(sel, st[:, sl] * scale - slope * dist, MASKED)
            m_old = m_ref[:, sl]
            m_new = jnp.maximum(m_old, jnp.max(s, axis=0, keepdims=True))
            alpha = jnp.exp(m_old - m_new)
            p = jnp.exp(s - m_new)
            l_ref[:, sl] = alpha * l_ref[:, sl] + jnp.sum(p, axis=0, keepdims=True)
            m_ref[:, sl] = m_new
            acc_ref[:, sl] = acc_ref[:, sl] * alpha
            ps.append(p.astype(BF16))
        acc_ref[...] += _dot_tn(vx, jnp.concatenate(ps, axis=1))
        return carry

    lax.fori_loop(0, nkc, attend_chunk, 0)

    for h in range(heads):
        sl = slice(h * tq, (h + 1) * tq)
        o_ref[:, h * LANES:(h + 1) * LANES] = (acc_ref[:, sl] / l_ref[:, sl]).T.astype(o_ref.dtype)


def _dsa(p1, p2, pos_row, pos_col, batch, seq, heads, idx_heads, topk, kv_off, kc):
    tq = LANES
    nq = seq // tq
    hw = heads * LANES
    iw = idx_heads * LANES
    assert iw % hw == 0
    kblk = kv_off // LANES
    misc_blk = p2.shape[1] // LANES - 1
    hg = min(8, idx_heads)
    kernel = functools.partial(_dsa_kernel, tq=tq, kc=kc, idx_heads=idx_heads, heads=heads,
                               topk=topk, hg=hg)
    return pl.pallas_call(
        kernel,
        grid=(batch, nq),
        in_specs=[pl.BlockSpec((tq, iw), lambda b, i: (b * nq + i, 0)),
                  pl.BlockSpec((tq, hw), lambda b, i: (b * nq + i, iw // hw)),
                  pl.BlockSpec((tq, LANES), lambda b, i: (b * nq + i, misc_blk)),
                  pl.BlockSpec((seq, LANES), lambda b, i: (b, kblk + 2)),
                  pl.BlockSpec((seq, LANES), lambda b, i: (b, kblk)),
                  pl.BlockSpec((seq, LANES), lambda b, i: (b, kblk + 1)),
                  pl.BlockSpec((1, 1, tq), lambda b, i: (b, 0, i)),
                  pl.BlockSpec((seq, 1), lambda b, i: (b, 0))],
        out_specs=pl.BlockSpec((tq, hw), lambda b, i: (b * nq + i, 0)),
        out_shape=jax.ShapeDtypeStruct((batch * seq, hw), BF16),
        scratch_shapes=[pltpu.VMEM((idx_heads * tq, LANES), BF16),
                        pltpu.VMEM((heads * tq, LANES), BF16),
                        pltpu.VMEM((idx_heads, tq), F32),
                        pltpu.VMEM((seq, tq), jnp.int32),
                        pltpu.VMEM((1, heads * tq), F32),
                        pltpu.VMEM((1, heads * tq), F32),
                        pltpu.VMEM((LANES, heads * tq), F32)],
        compiler_params=_params(("parallel", "arbitrary")),
        name="dsa",
    )(p1, p1, p2, p2, p2, p2, pos_row, pos_col)


def _tile(n, want):
    t = min(n, want)
    assert n % t == 0, (n, want)
    return t


def _block(x, c, positions, w_ada, b_ada, ln1_g, w_in, q_norm_g, kv_norm_g, w_uq, w_uk, w_uv,
           w_o, ln2_g, w_mlp_in, w_mlp_out, final_g, *, heads, idx_heads, b_heads):
    batch, seq, d = x.shape
    t = batch * seq
    q_lora = w_uq.shape[0]
    kv_lora = w_uk.shape[0]
    d_ff = w_mlp_in.shape[1]
    hw = heads * HEAD_DIM
    iw = idx_heads * HEAD_DIM
    bw = b_heads * HEAD_DIM
    topk = min(TOPK_MAX, seq // 4)
    assert (1 << CHUNK_SHIFT) == CHUNK and QK_ROPE + idx_heads <= LANES

    splits = (hw, HEAD_DIM, HEAD_DIM, iw, HEAD_DIM, idx_heads, q_lora, kv_lora, QK_ROPE)
    offs = [0]
    for s in splits:
        offs.append(offs[-1] + s)
    col = lambda k: w_in[:, offs[k]:offs[k + 1]]
    misc = jnp.concatenate(
        [col(8), col(5), jnp.zeros((d, LANES - QK_ROPE - idx_heads), w_in.dtype)], axis=1)
    w_pack = jnp.concatenate(
        [col(3), col(0), col(6), col(7), col(1), col(2), col(4), misc], axis=1).astype(BF16)
    width1 = hw + iw
    kv_off = q_lora + kv_lora

    wq = w_uq.reshape(q_lora, b_heads, HEAD_DIM + QK_ROPE)
    wqn = wq[:, :, :HEAD_DIM].reshape(q_lora, bw).astype(BF16)
    wqr = jnp.pad(wq[:, :, HEAD_DIM:], ((0, 0), (0, 0), (0, LANES - QK_ROPE))
                  ).reshape(q_lora, b_heads * LANES).astype(BF16)
    wuk = w_uk.astype(BF16)
    wuv = w_uv.astype(BF16)
    wo = w_o.astype(BF16)
    w1 = w_mlp_in.astype(BF16)
    w2 = w_mlp_out.astype(BF16)

    lane = jnp.arange(LANES)
    inv_row = (ROPE_THETA ** (-(2.0 * (lane % (QK_ROPE // 2))).astype(F32) / QK_ROPE)
               ).reshape(1, LANES)
    pos_row = positions.reshape(batch, 1, seq)
    pos_col = positions.reshape(t, 1)
    x2d = x.reshape(t, d)

    mod = _adaln_mod(c, w_ada, b_ada, _tile(6 * d, 512))
    shift1, scale1, gate1, shift2, scale2, gate2 = jnp.split(mod, 6, axis=-1)

    tm_ln = _tile(seq, 256)
    h = _ln_mod(x2d, ln1_g, scale1, shift1, seq, tm_ln)
    p1, p2 = _inproj(h, w_pack, width1, _tile(t, 1024), _tile(math.gcd(width1, w_pack.shape[1] - width1), 1024))

    out_a = _dsa(p1, p2, pos_row, pos_col, batch, seq, heads, idx_heads, topk, kv_off,
                 _tile(seq, 512))

    qn, qr, kn, vb, kr = _mla_proj(p2, pos_col, inv_row, q_norm_g.reshape(1, q_lora),
                                   kv_norm_g.reshape(1, kv_lora), wqn, wqr, wuk, wuv,
                                   q_lora, kv_lora, b_heads, _tile(t, 512))
    out_b = _mla_attention(qn, qr, kn, kr, vb, pos_row, pos_col, batch, seq, b_heads,
                           _tile(seq, 512), _tile(seq, 512))

    tm = _tile(seq, 1024)
    x1 = _mix_resid(out_a, out_b, wo, x2d, gate1, seq, tm, _tile(d, 1024))
    h2 = _ln_mod(x1, ln2_g, scale2, shift2, seq, tm_ln)
    hid = _mm_relu2(h2, w1, tm, _tile(d_ff, 1024))
    x2 = _mm_resid(hid, w2, x1, gate2, seq, tm, _tile(d, 1024), _tile(d_ff, 2048), "mlp_out")
    return _ln_final(x2, final_g, tm_ln).reshape(batch, seq, d)


def kernel(x, c, positions, w_ada, b_ada, ln1_g, w_in, q_norm_g, kv_norm_g, w_uq, w_uk, w_uv,
           w_o, ln2_g, w_mlp_in, w_mlp_out, final_g):
    assert w_ada.shape[0] == 1, "single-layer block"
    return _block(x, c, positions, w_ada[0], b_ada[0], ln1_g[0], w_in[0], q_norm_g[0],
                  kv_norm_g[0], w_uq[0], w_uk[0], w_uv[0], w_o[0], ln2_g[0], w_mlp_in[0],
                  w_mlp_out[0], final_g, heads=A_HEADS, idx_heads=IDX_HEADS, b_heads=B_HEADS)
```

```python
import functools
import math

import jax
import jax.numpy as jnp
from jax import lax
from jax.experimental import pallas as pl
from jax.experimental.pallas import tpu as pltpu

F32 = jnp.float32
BF16 = jnp.bfloat16

EPS = 1e-6
CHUNK = 64
CHUNK_SHIFT = 6
A_HEADS = 16
IDX_HEADS = 32
HEAD_DIM = 128
TOPK_MAX = 256
B_HEADS = 16
QK_ROPE = 64
ROPE_THETA = 10000.0

LANES = 128
LOG2E = math.log2(math.e)
MASKED = -1e30
INT_MIN = -(2 ** 31)
NEG_INF_KEY = -2139095041
VMEM_LIMIT = 56 * 1024 * 1024


def _params(semantics, vmem=VMEM_LIMIT):
    return pltpu.CompilerParams(dimension_semantics=semantics, vmem_limit_bytes=vmem)


def _dot_nt(a, b):
    return lax.dot_general(a, b, (((1,), (1,)), ((), ())), preferred_element_type=F32)


def _dot_tn(a, b):
    return lax.dot_general(a, b, (((0,), (0,)), ((), ())), preferred_element_type=F32)


def _mod_kernel(c_ref, w_ref, b_ref, o_ref):
    c = c_ref[...]
    s = c * (1.0 / (1.0 + jnp.exp(-c)))
    o_ref[...] = jnp.dot(s.astype(BF16), w_ref[...].astype(BF16),
                         preferred_element_type=F32) + b_ref[...]


def _adaln_mod(c, w_ada, b_ada, tn):
    b, d = c.shape
    n = w_ada.shape[1]
    rows = 8
    c_pad = jnp.zeros((rows, d), F32).at[:b].set(c)
    out = pl.pallas_call(
        _mod_kernel,
        grid=(n // tn,),
        in_specs=[pl.BlockSpec((rows, d), lambda j: (0, 0)),
                  pl.BlockSpec((d, tn), lambda j: (0, j)),
                  pl.BlockSpec((1, tn), lambda j: (0, j))],
        out_specs=pl.BlockSpec((rows, tn), lambda j: (0, j)),
        out_shape=jax.ShapeDtypeStruct((rows, n), F32),
        compiler_params=_params(("arbitrary",)),
        name="adaln_mod",
    )(c_pad, w_ada, b_ada.reshape(1, n))
    return out[:b]


def _ln_mod_kernel(x_ref, g_ref, sc_ref, sh_ref, o_ref):
    x = x_ref[...]
    y = x * lax.rsqrt(jnp.mean(x * x, axis=-1, keepdims=True) + EPS) * g_ref[...]
    o_ref[...] = (y * (1.0 + sc_ref[0]) + sh_ref[0]).astype(o_ref.dtype)


def _ln_kernel(x_ref, g_ref, o_ref):
    x = x_ref[...]
    o_ref[...] = (x * lax.rsqrt(jnp.mean(x * x, axis=-1, keepdims=True) + EPS)
                  * g_ref[...]).astype(o_ref.dtype)


def _ln_mod(x2d, g, scale, shift, seq, tm):
    t, d = x2d.shape
    nb = scale.shape[0]
    per = seq // tm
    return pl.pallas_call(
        _ln_mod_kernel,
        grid=(t // tm,),
        in_specs=[pl.BlockSpec((tm, d), lambda i: (i, 0)),
                  pl.BlockSpec((1, d), lambda i: (0, 0)),
                  pl.BlockSpec((1, 1, d), lambda i: (i // per, 0, 0)),
                  pl.BlockSpec((1, 1, d), lambda i: (i // per, 0, 0))],
        out_specs=pl.BlockSpec((tm, d), lambda i: (i, 0)),
        out_shape=jax.ShapeDtypeStruct((t, d), BF16),
        compiler_params=_params(("parallel",)),
        name="ln_mod",
    )(x2d, g.reshape(1, d), scale.reshape(nb, 1, d), shift.reshape(nb, 1, d))


def _ln_final(x2d, g, tm):
    t, d = x2d.shape
    return pl.pallas_call(
        _ln_kernel,
        grid=(t // tm,),
        in_specs=[pl.BlockSpec((tm, d), lambda i: (i, 0)),
                  pl.BlockSpec((1, d), lambda i: (0, 0))],
        out_specs=pl.BlockSpec((tm, d), lambda i: (i, 0)),
        out_shape=jax.ShapeDtypeStruct((t, d), F32),
        compiler_params=_params(("parallel",)),
        name="ln_final",
    )(x2d, g.reshape(1, d))


def _inproj_kernel(h_ref, w_ref, o1_ref, o2_ref, *, n1):
    j = pl.program_id(1)
    acc = jnp.dot(h_ref[...], w_ref[...], preferred_element_type=F32)

    @pl.when(j < n1)
    def _():
        o1_ref[...] = acc.astype(o1_ref.dtype)

    @pl.when(j >= n1)
    def _():
        o2_ref[...] = acc


def _inproj(h, w, width1, tm, tn):
    t, d = h.shape
    n = w.shape[1]
    n1 = width1 // tn
    return pl.pallas_call(
        functools.partial(_inproj_kernel, n1=n1),
        grid=(t // tm, n // tn),
        in_specs=[pl.BlockSpec((tm, d), lambda i, j: (i, 0)),
                  pl.BlockSpec((d, tn), lambda i, j: (0, j))],
        out_specs=[pl.BlockSpec((tm, tn), lambda i, j: (i, jnp.minimum(j, n1 - 1))),
                   pl.BlockSpec((tm, tn), lambda i, j: (i, jnp.maximum(j - n1, 0)))],
        out_shape=[jax.ShapeDtypeStruct((t, width1), BF16),
                   jax.ShapeDtypeStruct((t, n - width1), F32)],
        compiler_params=_params(("parallel", "arbitrary")),
        name="in_proj",
    )(h, w)


def _mm_relu2_kernel(a_ref, w_ref, o_ref):
    acc = jnp.dot(a_ref[...], w_ref[...], preferred_element_type=F32)
    r = jnp.maximum(acc, 0.0)
    o_ref[...] = (r * r).astype(o_ref.dtype)


def _mm_relu2(a, w, tm, tn):
    t, d = a.shape
    n = w.shape[1]
    return pl.pallas_call(
        _mm_relu2_kernel,
        grid=(t // tm, n // tn),
        in_specs=[pl.BlockSpec((tm, d), lambda i, j: (i, 0)),
                  pl.BlockSpec((d, tn), lambda i, j: (0, j))],
        out_specs=pl.BlockSpec((tm, tn), lambda i, j: (i, j)),
        out_shape=jax.ShapeDtypeStruct((t, n), BF16),
        compiler_params=_params(("parallel", "arbitrary")),
        name="mlp_in",
    )(a, w)


def _mm_resid_kernel(a_ref, w_ref, r_ref, g_ref, o_ref, *, nk):
    k = pl.program_id(2)
    part = lambda: jnp.dot(a_ref[...], w_ref[...], preferred_element_type=F32)
    if nk == 1:
        o_ref[...] = r_ref[...] + g_ref[0] * part()
        return

    @pl.when(k == 0)
    def _():
        o_ref[...] = part()

    @pl.when((k > 0) & (k < nk - 1))
    def _():
        o_ref[...] += part()

    @pl.when(k == nk - 1)
    def _():
        o_ref[...] = r_ref[...] + g_ref[0] * (o_ref[...] + part())


def _mm_resid(a, w, resid, gate, seq, tm, tn, tk, name):
    t, kdim = a.shape
    n = w.shape[1]
    nb = gate.shape[0]
    per = seq // tm
    nk = kdim // tk
    return pl.pallas_call(
        functools.partial(_mm_resid_kernel, nk=nk),
        grid=(t // tm, n // tn, nk),
        in_specs=[pl.BlockSpec((tm, tk), lambda i, j, k: (i, k)),
                  pl.BlockSpec((tk, tn), lambda i, j, k: (k, j)),
                  pl.BlockSpec((tm, tn), lambda i, j, k: (i, j)),
                  pl.BlockSpec((1, 1, tn), lambda i, j, k: (i // per, 0, j))],
        out_specs=pl.BlockSpec((tm, tn), lambda i, j, k: (i, j)),
        out_shape=jax.ShapeDtypeStruct((t, n), F32),
        compiler_params=_params(("parallel", "parallel", "arbitrary")),
        name=name,
    )(a, w, resid, gate.reshape(nb, 1, n))


def _mix_resid_kernel(a1_ref, a2_ref, w1_ref, w2_ref, r_ref, g_ref, o_ref):
    acc = jnp.dot(a1_ref[...], w1_ref[...], preferred_element_type=F32)
    acc += jnp.dot(a2_ref[...], w2_ref[...], preferred_element_type=F32)
    o_ref[...] = r_ref[...] + g_ref[0] * acc


def _mix_resid(a1, a2, w, resid, gate, seq, tm, tn):
    t, k1 = a1.shape
    k2 = a2.shape[1]
    assert k1 == k2
    n = w.shape[1]
    nb = gate.shape[0]
    per = seq // tm
    return pl.pallas_call(
        _mix_resid_kernel,
        grid=(t // tm, n // tn),
        in_specs=[pl.BlockSpec((tm, k1), lambda i, j: (i, 0)),
                  pl.BlockSpec((tm, k2), lambda i, j: (i, 0)),
                  pl.BlockSpec((k1, tn), lambda i, j: (0, j)),
                  pl.BlockSpec((k2, tn), lambda i, j: (1, j)),
                  pl.BlockSpec((tm, tn), lambda i, j: (i, j)),
                  pl.BlockSpec((1, 1, tn), lambda i, j: (i // per, 0, j))],
        out_specs=pl.BlockSpec((tm, tn), lambda i, j: (i, j)),
        out_shape=jax.ShapeDtypeStruct((t, n), F32),
        compiler_params=_params(("parallel", "arbitrary")),
        name="out_proj",
    )(a1, a2, w, w, resid, gate.reshape(nb, 1, n))


def _rope_tables(pos_col, inv_row):
    ang = pos_col.astype(F32) * inv_row
    lane = lax.broadcasted_iota(jnp.int32, ang.shape, 1)
    half = QK_ROPE // 2
    cos = jnp.cos(ang)
    sin = jnp.sin(ang)
    c = jnp.where(lane < QK_ROPE, cos, 0.0)
    s_lo = jnp.where(lane < half, -sin, 0.0)
    s_hi = jnp.where((lane >= half) & (lane < QK_ROPE), sin, 0.0)
    return c, s_lo, s_hi


def _rope_apply(x, c, s_lo, s_hi):
    half = QK_ROPE // 2
    return (x * c + pltpu.roll(x, LANES - half, axis=1) * s_lo
            + pltpu.roll(x, half, axis=1) * s_hi)


def _mla_proj_kernel(cq_ref, ckv_ref, misc_ref, pos_ref, inv_ref, gq_ref, gkv_ref,
                     wqn_ref, wqr_ref, wuk_ref, wuv_ref,
                     qn_ref, qr_ref, kn_ref, vb_ref, kr_ref, *, heads):
    cq = cq_ref[...]
    cqn = (cq * lax.rsqrt(jnp.mean(cq * cq, axis=-1, keepdims=True) + EPS)
           * gq_ref[...]).astype(BF16)
    ckv = ckv_ref[...]
    kvn = (ckv * lax.rsqrt(jnp.mean(ckv * ckv, axis=-1, keepdims=True) + EPS)
           * gkv_ref[...]).astype(BF16)
    c, s_lo, s_hi = _rope_tables(pos_ref[...], inv_ref[...])

    qn_ref[...] = jnp.dot(cqn, wqn_ref[...], preferred_element_type=F32).astype(BF16)
    qr = jnp.dot(cqn, wqr_ref[...], preferred_element_type=F32)
    for h in range(heads):
        sl = slice(h * LANES, (h + 1) * LANES)
        qr_ref[:, sl] = _rope_apply(qr[:, sl], c, s_lo, s_hi).astype(BF16)
    kn_ref[...] = jnp.dot(kvn, wuk_ref[...], preferred_element_type=F32).astype(BF16)
    vb_ref[...] = jnp.dot(kvn, wuv_ref[...], preferred_element_type=F32).astype(BF16)
    kr_ref[...] = _rope_apply(misc_ref[...], c, s_lo, s_hi).astype(BF16)


def _mla_proj(p2, pos_col, inv_row, gq, gkv, wqn, wqr, wuk, wuv, q_lora, kv_lora, heads, tm):
    t = p2.shape[0]
    hw = heads * LANES
    misc_blk = p2.shape[1] // LANES - 1
    const = lambda i: (0, 0)
    return pl.pallas_call(
        functools.partial(_mla_proj_kernel, heads=heads),
        grid=(t // tm,),
        in_specs=[pl.BlockSpec((tm, q_lora), lambda i: (i, 0)),
                  pl.BlockSpec((tm, kv_lora), lambda i: (i, q_lora // kv_lora)),
                  pl.BlockSpec((tm, LANES), lambda i: (i, misc_blk)),
                  pl.BlockSpec((tm, 1), lambda i: (i, 0)),
                  pl.BlockSpec((1, LANES), const),
                  pl.BlockSpec((1, q_lora), const),
                  pl.BlockSpec((1, kv_lora), const),
                  pl.BlockSpec((q_lora, hw), const),
                  pl.BlockSpec((q_lora, hw), const),
                  pl.BlockSpec((kv_lora, hw), const),
                  pl.BlockSpec((kv_lora, hw), const)],
        out_specs=[pl.BlockSpec((tm, hw), lambda i: (i, 0))] * 4
                  + [pl.BlockSpec((tm, LANES), lambda i: (i, 0))],
        out_shape=[jax.ShapeDtypeStruct((t, hw), BF16)] * 4
                  + [jax.ShapeDtypeStruct((t, LANES), BF16)],
        compiler_params=_params(("parallel",)),
        name="mla_proj",
    )(p2, p2, p2, pos_col, inv_row, gq, gkv, wqn, wqr, wuk, wuv)


def _mla_attn_kernel(it_ref, jt_ref, qn_ref, qr_ref, kn_ref, kr_ref, v_ref, qpos_ref, kpos_ref,
                     o_ref, m_ref, l_ref, acc_ref, *, group, tq, tk):
    p = pl.program_id(2)
    i = it_ref[p]
    j = jt_ref[p]

    @pl.when(j == 0)
    def _():
        m_ref[...] = jnp.full(m_ref.shape, MASKED, F32)
        l_ref[...] = jnp.zeros(l_ref.shape, F32)
        acc_ref[...] = jnp.zeros(acc_ref.shape, F32)

    def step(masked):
        kr = kr_ref[...]
        if masked:
            adm = (kpos_ref[...] >> CHUNK_SHIFT) <= (qpos_ref[0] >> CHUNK_SHIFT)
        def scores(g):
            sl = slice(g * LANES, (g + 1) * LANES)
            q2 = jnp.concatenate([qn_ref[:, sl], qr_ref[:, sl]], axis=1)
            k2 = jnp.concatenate([kn_ref[:, sl], kr], axis=1)
            return _dot_nt(k2, q2)

        depth = 4
        ahead = [scores(g) for g in range(min(depth, group))]
        for g in range(group):
            sl = slice(g * LANES, (g + 1) * LANES)
            st = ahead.pop(0)
            if g + depth < group:
                ahead.append(scores(g + depth))
            if masked:
                st = jnp.where(adm, st, MASKED)
            m_old = m_ref[g:g + 1, :]
            m_new = jnp.maximum(m_old, jnp.max(st, axis=0, keepdims=True))
            alpha = jnp.exp2(m_old - m_new)
            pt = jnp.exp2(st - m_new)
            l_ref[g:g + 1, :] = alpha * l_ref[g:g + 1, :] + jnp.sum(pt, axis=0, keepdims=True)
            m_ref[g:g + 1, :] = m_new
            acc_ref[sl, :] = alpha * acc_ref[sl, :] + _dot_tn(v_ref[:, sl], pt.astype(BF16))

    reaches = (j + 1) * tk > i * tq

    @pl.when(reaches)
    def _():
        step(True)

    @pl.when(jnp.logical_not(reaches))
    def _():
        step(False)

    @pl.when(j == ((i + 1) * tq - 1) // tk)
    def _():
        for g in range(group):
            sl = slice(g * LANES, (g + 1) * LANES)
            o_ref[:, sl] = (acc_ref[sl, :] / l_ref[g:g + 1, :]).T.astype(o_ref.dtype)


def _mla_attention(qn, qr, kn, kr, vb, pos_row, pos_col, batch, seq, heads, tq, tk, group):
    nq = seq // tq
    nk = seq // tk
    gw = group * LANES
    pairs = [(i, j) for i in range(nq) for j in range(nk) if j * tk < (i + 1) * tq]
    it = jnp.asarray([p[0] for p in pairs], jnp.int32)
    jt = jnp.asarray([p[1] for p in pairs], jnp.int32)
    qmap = lambda b, h, p, it, jt: (b * nq + it[p], h)
    kmap = lambda b, h, p, it, jt: (b * nk + jt[p], h)
    grid_spec = pltpu.PrefetchScalarGridSpec(
        num_scalar_prefetch=2,
        grid=(batch, heads // group, len(pairs)),
        in_specs=[pl.BlockSpec((tq, gw), qmap),
                  pl.BlockSpec((tq, gw), qmap),
                  pl.BlockSpec((tk, gw), kmap),
                  pl.BlockSpec((tk, LANES), lambda b, h, p, it, jt: (b * nk + jt[p], 0)),
                  pl.BlockSpec((tk, gw), kmap),
                  pl.BlockSpec((1, 1, tq), lambda b, h, p, it, jt: (b, 0, it[p])),
                  pl.BlockSpec((tk, 1), lambda b, h, p, it, jt: (b * nk + jt[p], 0))],
        out_specs=pl.BlockSpec((tq, gw), qmap),
        scratch_shapes=[pltpu.VMEM((group, tq), F32), pltpu.VMEM((group, tq), F32),
                        pltpu.VMEM((gw, tq), F32)],
    )
    return pl.pallas_call(
        functools.partial(_mla_attn_kernel, group=group, tq=tq, tk=tk),
        grid_spec=grid_spec,
        out_shape=jax.ShapeDtypeStruct((batch * seq, heads * LANES), BF16),
        compiler_params=_params(("parallel", "parallel", "arbitrary")),
        name="mla_attn",
    )(it, jt, qn, qr, kn, kr, vb, pos_row, pos_col)


def _tree_sum8(x):
    parts = [x[r:r + 8] for r in range(0, x.shape[0], 8)]
    while len(parts) > 1:
        nxt = [parts[a] + parts[a + 1] for a in range(0, len(parts) - 1, 2)]
        if len(parts) % 2:
            nxt.append(parts[-1])
        parts = nxt
    return parts[0]


def _dsa_kernel(qi_ref, qa_ref, misc_ref, kidx_ref, ka_ref, va_ref, qpos_ref, kpos_ref, o_ref,
                qis_ref, qas_ref, wt_ref, skey_ref, m_ref, l_ref, acc_ref,
                *, tq, kc, idx_heads, heads, topk, hg):
    i = pl.program_id(1)
    nkc = ((i + 1) * tq + kc - 1) // kc

    for h in range(idx_heads):
        qis_ref[h * tq:(h + 1) * tq, :] = qi_ref[:, h * LANES:(h + 1) * LANES]
    for h in range(heads):
        qas_ref[h * tq:(h + 1) * tq, :] = qa_ref[:, h * LANES:(h + 1) * LANES]
    wt_ref[...] = misc_ref[...].T[QK_ROPE:QK_ROPE + idx_heads, :] * (
        (idx_heads ** -0.5) * (HEAD_DIM ** -0.5))
    qpos = qpos_ref[0]
    qchunk = qpos >> CHUNK_SHIFT

    def score_chunk(c, carry):
        r0 = pl.multiple_of(c * kc, kc)
        kx = kidx_ref[pl.ds(r0, kc), :].astype(BF16)
        score = jnp.zeros((kc, tq), F32)
        for g in range(idx_heads // hg):
            lt = _dot_nt(kx, qis_ref[g * hg * tq:(g + 1) * hg * tq, :])
            for hh in range(hg):
                h = g * hg + hh
                score += jnp.maximum(lt[:, hh * tq:(hh + 1) * tq], 0.0) * wt_ref[h:h + 1, :]
        adm = (kpos_ref[pl.ds(r0, kc), :] >> CHUNK_SHIFT) <= qchunk
        bits = pltpu.bitcast(jnp.where(adm, score, -jnp.inf), jnp.int32)
        skey_ref[pl.ds(r0, kc), :] = bits ^ ((bits >> 31) & 0x7FFFFFFF)
        return carry

    lax.fori_loop(0, nkc, score_chunk, 0)

    def count_ge(thr):
        def body(c, cnt):
            r0 = pl.multiple_of(c * kc, kc)
            hit = jnp.where(skey_ref[pl.ds(r0, kc), :] >= thr, 1.0, 0.0)
            return cnt + _tree_sum8(hit)
        cnt8 = lax.fori_loop(0, nkc, body, jnp.zeros((8, tq), F32))
        return jnp.sum(cnt8, axis=0, keepdims=True)

    need = float(topk)
    thr = jnp.where(count_ge(jnp.zeros((1, tq), jnp.int32)) >= need, 0, INT_MIN)

    def bit_step(it, thr):
        cand = thr + lax.shift_left(jnp.int32(1), 30 - it)
        return jnp.where(count_ge(cand) >= need, cand, thr)

    thr = lax.fori_loop(0, 31, bit_step, thr)
    thr = jnp.maximum(thr, NEG_INF_KEY + 1)

    m_ref[...] = jnp.full(m_ref.shape, MASKED, F32)
    l_ref[...] = jnp.zeros(l_ref.shape, F32)
    acc_ref[...] = jnp.zeros(acc_ref.shape, F32)

    def attend_chunk(c, carry):
        r0 = pl.multiple_of(c * kc, kc)
        kx = ka_ref[pl.ds(r0, kc), :].astype(BF16)
        vx = va_ref[pl.ds(r0, kc), :].astype(BF16)
        sel = skey_ref[pl.ds(r0, kc), :] >= thr
        dist = jnp.abs(kpos_ref[pl.ds(r0, kc), :] - qpos).astype(F32)
        st = _dot_nt(kx, qas_ref[...])
        ps = []
        for h in range(heads):
            sl = slice(h * tq, (h + 1) * tq)
            slope = LOG2E * 2.0 ** (-8.0 * (h + 1) / heads)
            s = jnp.where(sel, st[:, sl] - slope * dist, MASKED)
            m_old = m_ref[:, sl]
            m_new = jnp.maximum(m_old, jnp.max(s, axis=0, keepdims=True))
            alpha = jnp.exp2(m_old - m_new)
            p = jnp.exp2(s - m_new)
            l_ref[:, sl] = alpha * l_ref[:, sl] + jnp.sum(p, axis=0, keepdims=True)
            m_ref[:, sl] = m_new
            acc_ref[:, sl] = acc_ref[:, sl] * alpha
            ps.append(p.astype(BF16))
        acc_ref[...] += _dot_tn(vx, jnp.concatenate(ps, axis=1))
        return carry

    lax.fori_loop(0, nkc, attend_chunk, 0)

    for h in range(heads):
        sl = slice(h * tq, (h + 1) * tq)
        o_ref[:, h * LANES:(h + 1) * LANES] = (acc_ref[:, sl] / l_ref[:, sl]).T.astype(o_ref.dtype)


def _dsa(p1, p2, pos_row, pos_col, batch, seq, heads, idx_heads, topk, kv_off, kc):
    tq = LANES
    nq = seq // tq
    hw = heads * LANES
    iw = idx_heads * LANES
    assert iw % hw == 0
    kblk = kv_off // LANES
    misc_blk = p2.shape[1] // LANES - 1
    hg = min(8, idx_heads)
    kernel = functools.partial(_dsa_kernel, tq=tq, kc=kc, idx_heads=idx_heads, heads=heads,
                               topk=topk, hg=hg)
    return pl.pallas_call(
        kernel,
        grid=(batch, nq),
        in_specs=[pl.BlockSpec((tq, iw), lambda b, i: (b * nq + i, 0)),
                  pl.BlockSpec((tq, hw), lambda b, i: (b * nq + i, iw // hw)),
                  pl.BlockSpec((tq, LANES), lambda b, i: (b * nq + i, misc_blk)),
                  pl.BlockSpec((seq, LANES), lambda b, i: (b, kblk + 2)),
                  pl.BlockSpec((seq, LANES), lambda b, i: (b, kblk)),
                  pl.BlockSpec((seq, LANES), lambda b, i: (b, kblk + 1)),
                  pl.BlockSpec((1, 1, tq), lambda b, i: (b, 0, i)),
                  pl.BlockSpec((seq, 1), lambda b, i: (b, 0))],
        out_specs=pl.BlockSpec((tq, hw), lambda b, i: (b * nq + i, 0)),
        out_shape=jax.ShapeDtypeStruct((batch * seq, hw), BF16),
        scratch_shapes=[pltpu.VMEM((idx_heads * tq, LANES), BF16),
                        pltpu.VMEM((heads * tq, LANES), BF16),
                        pltpu.VMEM((idx_heads, tq), F32),
                        pltpu.VMEM((seq, tq), jnp.int32),
                        pltpu.VMEM((1, heads * tq), F32),
                        pltpu.VMEM((1, heads * tq), F32),
                        pltpu.VMEM((LANES, heads * tq), F32)],
        compiler_params=_params(("parallel", "arbitrary")),
        name="dsa",
    )(p1, p1, p2, p2, p2, p2, pos_row, pos_col)


def _tile(n, want):
    t = min(n, want)
    assert n % t == 0, (n, want)
    return t


def _block(x, c, positions, w_ada, b_ada, ln1_g, w_in, q_norm_g, kv_norm_g, w_uq, w_uk, w_uv,
           w_o, ln2_g, w_mlp_in, w_mlp_out, final_g, *, heads, idx_heads, b_heads):
    batch, seq, d = x.shape
    t = batch * seq
    q_lora = w_uq.shape[0]
    kv_lora = w_uk.shape[0]
    d_ff = w_mlp_in.shape[1]
    hw = heads * HEAD_DIM
    iw = idx_heads * HEAD_DIM
    bw = b_heads * HEAD_DIM
    topk = min(TOPK_MAX, seq // 4)
    assert (1 << CHUNK_SHIFT) == CHUNK and QK_ROPE + idx_heads <= LANES

    splits = (hw, HEAD_DIM, HEAD_DIM, iw, HEAD_DIM, idx_heads, q_lora, kv_lora, QK_ROPE)
    offs = [0]
    for s in splits:
        offs.append(offs[-1] + s)
    col = lambda k: w_in[:, offs[k]:offs[k + 1]]
    misc = jnp.concatenate(
        [col(8), col(5), jnp.zeros((d, LANES - QK_ROPE - idx_heads), w_in.dtype)], axis=1)
    w_pack = jnp.concatenate(
        [col(3), col(0) * (HEAD_DIM ** -0.5 * LOG2E), col(6), col(7), col(1), col(2), col(4),
         misc], axis=1).astype(BF16)
    width1 = hw + iw
    kv_off = q_lora + kv_lora

    wq = w_uq.reshape(q_lora, b_heads, HEAD_DIM + QK_ROPE) * ((HEAD_DIM + QK_ROPE) ** -0.5 * LOG2E)
    wqn = wq[:, :, :HEAD_DIM].reshape(q_lora, bw).astype(BF16)
    wqr = jnp.pad(wq[:, :, HEAD_DIM:], ((0, 0), (0, 0), (0, LANES - QK_ROPE))
                  ).reshape(q_lora, b_heads * LANES).astype(BF16)
    wuk = w_uk.astype(BF16)
    wuv = w_uv.astype(BF16)
    wo = w_o.astype(BF16)
    w1 = w_mlp_in.astype(BF16)
    w2 = w_mlp_out.astype(BF16)

    lane = jnp.arange(LANES)
    inv_row = (ROPE_THETA ** (-(2.0 * (lane % (QK_ROPE // 2))).astype(F32) / QK_ROPE)
               ).reshape(1, LANES)
    pos_row = positions.reshape(batch, 1, seq)
    pos_col = positions.reshape(t, 1)
    x2d = x.reshape(t, d)

    mod = _adaln_mod(c, w_ada, b_ada, _tile(6 * d, 512))
    shift1, scale1, gate1, shift2, scale2, gate2 = jnp.split(mod, 6, axis=-1)

    tm_ln = _tile(seq, 256)
    h = _ln_mod(x2d, ln1_g, scale1, shift1, seq, tm_ln)
    p1, p2 = _inproj(h, w_pack, width1, _tile(t, 1024), _tile(math.gcd(width1, w_pack.shape[1] - width1), 1024))

    out_a = _dsa(p1, p2, pos_row, pos_col, batch, seq, heads, idx_heads, topk, kv_off,
                 _tile(seq, 512))

    qn, qr, kn, vb, kr = _mla_proj(p2, pos_col, inv_row, q_norm_g.reshape(1, q_lora),
                                   kv_norm_g.reshape(1, kv_lora), wqn, wqr, wuk, wuv,
                                   q_lora, kv_lora, b_heads, _tile(t, 512))
    out_b = _mla_attention(qn, qr, kn, kr, vb, pos_row, pos_col, batch, seq, b_heads,
                           _tile(seq, 512), _tile(seq, 512), min(b_heads, 8))

    tm = _tile(seq, 1024)
    x1 = _mix_resid(out_a, out_b, wo, x2d, gate1, seq, tm, _tile(d, 1024))
    h2 = _ln_mod(x1, ln2_g, scale2, shift2, seq, tm_ln)
    hid = _mm_relu2(h2, w1, tm, _tile(d_ff, 1024))
    x2 = _mm_resid(hid, w2, x1, gate2, seq, tm, _tile(d, 1024), _tile(d_ff, 2048), "mlp_out")
    return _ln_final(x2, final_g, tm_ln).reshape(batch, seq, d)


def kernel(x, c, positions, w_ada, b_ada, ln1_g, w_in, q_norm_g, kv_norm_g, w_uq, w_uk, w_uv,
           w_o, ln2_g, w_mlp_in, w_mlp_out, final_g):
    assert w_ada.shape[0] == 1, "single-layer block"
    return _block(x, c, positions, w_ada[0], b_ada[0], ln1_g[0], w_in[0], q_norm_g[0],
                  kv_norm_g[0], w_uq[0], w_uk[0], w_uv[0], w_o[0], ln2_g[0], w_mlp_in[0],
                  w_mlp_out[0], final_g, heads=A_HEADS, idx_heads=IDX_HEADS, b_heads=B_HEADS)
```

```python
import functools
import math

import jax
import jax.numpy as jnp
import numpy as np
from jax import lax
from jax.experimental import pallas as pl
from jax.experimental.pallas import tpu as pltpu

F32 = jnp.float32
BF16 = jnp.bfloat16

EPS = 1e-6
CHUNK = 64
CHUNK_SHIFT = 6
A_HEADS = 16
IDX_HEADS = 32
HEAD_DIM = 128
TOPK_MAX = 256
B_HEADS = 16
QK_ROPE = 64
ROPE_THETA = 10000.0

LANES = 128
LOG2E = math.log2(math.e)
MASKED = -1e30
INT_MIN = -(2 ** 31)
NEG_INF_KEY = -2139095041
VMEM_LIMIT = 56 * 1024 * 1024


def _params(semantics, vmem=VMEM_LIMIT):
    return pltpu.CompilerParams(dimension_semantics=semantics, vmem_limit_bytes=vmem)


def _dot_nt(a, b):
    return lax.dot_general(a, b, (((1,), (1,)), ((), ())), preferred_element_type=F32)


def _dot_tn(a, b):
    return lax.dot_general(a, b, (((0,), (0,)), ((), ())), preferred_element_type=F32)


def _mod_kernel(c_ref, w_ref, b_ref, o_ref):
    c = c_ref[...]
    s = c * (1.0 / (1.0 + jnp.exp(-c)))
    o_ref[...] = jnp.dot(s.astype(BF16), w_ref[...].astype(BF16),
                         preferred_element_type=F32) + b_ref[...]


def _adaln_mod(c, w_ada, b_ada, tn):
    b, d = c.shape
    n = w_ada.shape[1]
    rows = 8
    c_pad = jnp.zeros((rows, d), F32).at[:b].set(c)
    out = pl.pallas_call(
        _mod_kernel,
        grid=(n // tn,),
        in_specs=[pl.BlockSpec((rows, d), lambda j: (0, 0)),
                  pl.BlockSpec((d, tn), lambda j: (0, j)),
                  pl.BlockSpec((1, tn), lambda j: (0, j))],
        out_specs=pl.BlockSpec((rows, tn), lambda j: (0, j)),
        out_shape=jax.ShapeDtypeStruct((rows, n), F32),
        compiler_params=_params(("arbitrary",)),
        name="adaln_mod",
    )(c_pad, w_ada, b_ada.reshape(1, n))
    return out[:b]


def _ln_mod_kernel(x_ref, g_ref, sc_ref, sh_ref, o_ref):
    x = x_ref[...]
    y = x * lax.rsqrt(jnp.mean(x * x, axis=-1, keepdims=True) + EPS) * g_ref[...]
    o_ref[...] = (y * (1.0 + sc_ref[0]) + sh_ref[0]).astype(o_ref.dtype)


def _ln_kernel(x_ref, g_ref, o_ref):
    x = x_ref[...]
    o_ref[...] = (x * lax.rsqrt(jnp.mean(x * x, axis=-1, keepdims=True) + EPS)
                  * g_ref[...]).astype(o_ref.dtype)


def _ln_mod(x2d, g, scale, shift, seq, tm):
    t, d = x2d.shape
    nb = scale.shape[0]
    per = seq // tm
    return pl.pallas_call(
        _ln_mod_kernel,
        grid=(t // tm,),
        in_specs=[pl.BlockSpec((tm, d), lambda i: (i, 0)),
                  pl.BlockSpec((1, d), lambda i: (0, 0)),
                  pl.BlockSpec((1, 1, d), lambda i: (i // per, 0, 0)),
                  pl.BlockSpec((1, 1, d), lambda i: (i // per, 0, 0))],
        out_specs=pl.BlockSpec((tm, d), lambda i: (i, 0)),
        out_shape=jax.ShapeDtypeStruct((t, d), BF16),
        compiler_params=_params(("parallel",)),
        name="ln_mod",
    )(x2d, g.reshape(1, d), scale.reshape(nb, 1, d), shift.reshape(nb, 1, d))


def _ln_final(x2d, g, tm):
    t, d = x2d.shape
    return pl.pallas_call(
        _ln_kernel,
        grid=(t // tm,),
        in_specs=[pl.BlockSpec((tm, d), lambda i: (i, 0)),
                  pl.BlockSpec((1, d), lambda i: (0, 0))],
        out_specs=pl.BlockSpec((tm, d), lambda i: (i, 0)),
        out_shape=jax.ShapeDtypeStruct((t, d), F32),
        compiler_params=_params(("parallel",)),
        name="ln_final",
    )(x2d, g.reshape(1, d))


def _pack_kernel(w_ref, o_ref, *, pieces, ncols):
    dst = 0
    for src, width, scale in pieces:
        lo = (src // LANES) * LANES
        hi = min(-(-(src + width) // LANES) * LANES, ncols)
        v = w_ref[:, lo:hi][:, src - lo:src - lo + width]
        if scale != 1.0:
            v = v * scale
        o_ref[:, dst:dst + width] = v.astype(o_ref.dtype)
        dst += width
    if dst < o_ref.shape[1]:
        o_ref[:, dst:] = jnp.zeros((o_ref.shape[0], o_ref.shape[1] - dst), o_ref.dtype)


def _pack_weights(w, pieces, width, tr):
    d, ncols = w.shape
    return pl.pallas_call(
        functools.partial(_pack_kernel, pieces=pieces, ncols=ncols),
        grid=(d // tr,),
        in_specs=[pl.BlockSpec((tr, ncols), lambda i: (i, 0))],
        out_specs=pl.BlockSpec((tr, width), lambda i: (i, 0)),
        out_shape=jax.ShapeDtypeStruct((d, width), BF16),
        compiler_params=_params(("parallel",)),
        name="pack_w_in",
    )(w)


def _inproj_kernel(h_ref, w_ref, o1_ref, o2_ref, *, n1):
    j = pl.program_id(1)
    acc = jnp.dot(h_ref[...], w_ref[...], preferred_element_type=F32)

    @pl.when(j < n1)
    def _():
        o1_ref[...] = acc.astype(o1_ref.dtype)

    @pl.when(j >= n1)
    def _():
        o2_ref[...] = acc


def _inproj(h, w, width1, tm, tn):
    t, d = h.shape
    n = w.shape[1]
    n1 = width1 // tn
    return pl.pallas_call(
        functools.partial(_inproj_kernel, n1=n1),
        grid=(t // tm, n // tn),
        in_specs=[pl.BlockSpec((tm, d), lambda i, j: (i, 0)),
                  pl.BlockSpec((d, tn), lambda i, j: (0, j))],
        out_specs=[pl.BlockSpec((tm, tn), lambda i, j: (i, jnp.minimum(j, n1 - 1))),
                   pl.BlockSpec((tm, tn), lambda i, j: (i, jnp.maximum(j - n1, 0)))],
        out_shape=[jax.ShapeDtypeStruct((t, width1), BF16),
                   jax.ShapeDtypeStruct((t, n - width1), F32)],
        compiler_params=_params(("parallel", "arbitrary")),
        name="in_proj",
    )(h, w)


def _mm_relu2_kernel(a_ref, w_ref, o_ref):
    acc = jnp.dot(a_ref[...], w_ref[...], preferred_element_type=F32)
    r = jnp.maximum(acc, 0.0)
    o_ref[...] = (r * r).astype(o_ref.dtype)


def _mm_relu2(a, w, tm, tn):
    t, d = a.shape
    n = w.shape[1]
    return pl.pallas_call(
        _mm_relu2_kernel,
        grid=(t // tm, n // tn),
        in_specs=[pl.BlockSpec((tm, d), lambda i, j: (i, 0)),
                  pl.BlockSpec((d, tn), lambda i, j: (0, j))],
        out_specs=pl.BlockSpec((tm, tn), lambda i, j: (i, j)),
        out_shape=jax.ShapeDtypeStruct((t, n), BF16),
        compiler_params=_params(("parallel", "arbitrary")),
        name="mlp_in",
    )(a, w)


def _mm_resid_kernel(a_ref, w_ref, r_ref, g_ref, o_ref, *, nk):
    k = pl.program_id(2)
    part = lambda: jnp.dot(a_ref[...], w_ref[...], preferred_element_type=F32)
    if nk == 1:
        o_ref[...] = r_ref[...] + g_ref[0] * part()
        return

    @pl.when(k == 0)
    def _():
        o_ref[...] = part()

    @pl.when((k > 0) & (k < nk - 1))
    def _():
        o_ref[...] += part()

    @pl.when(k == nk - 1)
    def _():
        o_ref[...] = r_ref[...] + g_ref[0] * (o_ref[...] + part())


def _mm_resid(a, w, resid, gate, seq, tm, tn, tk, name):
    t, kdim = a.shape
    n = w.shape[1]
    nb = gate.shape[0]
    per = seq // tm
    nk = kdim // tk
    return pl.pallas_call(
        functools.partial(_mm_resid_kernel, nk=nk),
        grid=(t // tm, n // tn, nk),
        in_specs=[pl.BlockSpec((tm, tk), lambda i, j, k: (i, k)),
                  pl.BlockSpec((tk, tn), lambda i, j, k: (k, j)),
                  pl.BlockSpec((tm, tn), lambda i, j, k: (i, j)),
                  pl.BlockSpec((1, 1, tn), lambda i, j, k: (i // per, 0, j))],
        out_specs=pl.BlockSpec((tm, tn), lambda i, j, k: (i, j)),
        out_shape=jax.ShapeDtypeStruct((t, n), F32),
        compiler_params=_params(("parallel", "parallel", "arbitrary")),
        name=name,
    )(a, w, resid, gate.reshape(nb, 1, n))


def _mix_resid_kernel(a1_ref, a2_ref, w1_ref, w2_ref, r_ref, g_ref, o_ref):
    acc = jnp.dot(a1_ref[...], w1_ref[...], preferred_element_type=F32)
    acc += jnp.dot(a2_ref[...], w2_ref[...], preferred_element_type=F32)
    o_ref[...] = r_ref[...] + g_ref[0] * acc


def _mix_resid(a1, a2, w, resid, gate, seq, tm, tn):
    t, k1 = a1.shape
    k2 = a2.shape[1]
    assert k1 == k2
    n = w.shape[1]
    nb = gate.shape[0]
    per = seq // tm
    return pl.pallas_call(
        _mix_resid_kernel,
        grid=(t // tm, n // tn),
        in_specs=[pl.BlockSpec((tm, k1), lambda i, j: (i, 0)),
                  pl.BlockSpec((tm, k2), lambda i, j: (i, 0)),
                  pl.BlockSpec((k1, tn), lambda i, j: (0, j)),
                  pl.BlockSpec((k2, tn), lambda i, j: (1, j)),
                  pl.BlockSpec((tm, tn), lambda i, j: (i, j)),
                  pl.BlockSpec((1, 1, tn), lambda i, j: (i // per, 0, j))],
        out_specs=pl.BlockSpec((tm, tn), lambda i, j: (i, j)),
        out_shape=jax.ShapeDtypeStruct((t, n), F32),
        compiler_params=_params(("parallel", "arbitrary")),
        name="out_proj",
    )(a1, a2, w, w, resid, gate.reshape(nb, 1, n))


def _rope_tables(pos_col, inv_row):
    ang = pos_col.astype(F32) * inv_row
    lane = lax.broadcasted_iota(jnp.int32, ang.shape, 1)
    half = QK_ROPE // 2
    cos = jnp.cos(ang)
    sin = jnp.sin(ang)
    c = jnp.where(lane < QK_ROPE, cos, 0.0)
    s_lo = jnp.where(lane < half, -sin, 0.0)
    s_hi = jnp.where((lane >= half) & (lane < QK_ROPE), sin, 0.0)
    return c, s_lo, s_hi


def _rope_apply(x, c, s_lo, s_hi):
    half = QK_ROPE // 2
    return (x * c + pltpu.roll(x, LANES - half, axis=1) * s_lo
            + pltpu.roll(x, half, axis=1) * s_hi)


def _mla_proj_kernel(cq_ref, ckv_ref, misc_ref, pos_ref, inv_ref, gq_ref, gkv_ref,
                     wqn_ref, wqr_ref, wuk_ref, wuv_ref,
                     qn_ref, qr_ref, kn_ref, vb_ref, kr_ref, *, heads):
    cq = cq_ref[...]
    cqn = (cq * lax.rsqrt(jnp.mean(cq * cq, axis=-1, keepdims=True) + EPS)
           * gq_ref[...]).astype(BF16)
    ckv = ckv_ref[...]
    kvn = (ckv * lax.rsqrt(jnp.mean(ckv * ckv, axis=-1, keepdims=True) + EPS)
           * gkv_ref[...]).astype(BF16)
    c, s_lo, s_hi = _rope_tables(pos_ref[...], inv_ref[...])

    qn_ref[...] = jnp.dot(cqn, wqn_ref[...], preferred_element_type=F32).astype(BF16)
    qr = jnp.dot(cqn, wqr_ref[...], preferred_element_type=F32)
    for h in range(heads):
        sl = slice(h * LANES, (h + 1) * LANES)
        qr_ref[:, sl] = _rope_apply(qr[:, sl], c, s_lo, s_hi).astype(BF16)
    kn_ref[...] = jnp.dot(kvn, wuk_ref[...], preferred_element_type=F32).astype(BF16)
    vb_ref[...] = jnp.dot(kvn, wuv_ref[...], preferred_element_type=F32).astype(BF16)
    kr_ref[...] = _rope_apply(misc_ref[...], c, s_lo, s_hi).astype(BF16)


def _mla_proj(p2, pos_col, inv_row, gq, gkv, wqn, wqr, wuk, wuv, q_lora, kv_lora, heads, tm):
    t = p2.shape[0]
    hw = heads * LANES
    misc_blk = p2.shape[1] // LANES - 1
    const = lambda i: (0, 0)
    return pl.pallas_call(
        functools.partial(_mla_proj_kernel, heads=heads),
        grid=(t // tm,),
        in_specs=[pl.BlockSpec((tm, q_lora), lambda i: (i, 0)),
                  pl.BlockSpec((tm, kv_lora), lambda i: (i, q_lora // kv_lora)),
                  pl.BlockSpec((tm, LANES), lambda i: (i, misc_blk)),
                  pl.BlockSpec((tm, 1), lambda i: (i, 0)),
                  pl.BlockSpec((1, LANES), const),
                  pl.BlockSpec((1, q_lora), const),
                  pl.BlockSpec((1, kv_lora), const),
                  pl.BlockSpec((q_lora, hw), const),
                  pl.BlockSpec((q_lora, hw), const),
                  pl.BlockSpec((kv_lora, hw), const),
                  pl.BlockSpec((kv_lora, hw), const)],
        out_specs=[pl.BlockSpec((tm, hw), lambda i: (i, 0))] * 4
                  + [pl.BlockSpec((tm, LANES), lambda i: (i, 0))],
        out_shape=[jax.ShapeDtypeStruct((t, hw), BF16)] * 4
                  + [jax.ShapeDtypeStruct((t, LANES), BF16)],
        compiler_params=_params(("parallel",)),
        name="mla_proj",
    )(p2, p2, p2, pos_col, inv_row, gq, gkv, wqn, wqr, wuk, wuv)


def _mla_attn_kernel(it_ref, jt_ref, qn_ref, qr_ref, kn_ref, kr_ref, v_ref, qpos_ref, kpos_ref,
                     o_ref, m_ref, l_ref, acc_ref, *, group, tq, tk):
    p = pl.program_id(2)
    i = it_ref[p]
    j = jt_ref[p]

    @pl.when(j == 0)
    def _():
        m_ref[...] = jnp.full(m_ref.shape, MASKED, F32)
        l_ref[...] = jnp.zeros(l_ref.shape, F32)
        acc_ref[...] = jnp.zeros(acc_ref.shape, F32)

    def step(masked):
        kr = kr_ref[...]
        if masked:
            adm = (kpos_ref[...] >> CHUNK_SHIFT) <= (qpos_ref[0] >> CHUNK_SHIFT)
        def scores(g):
            sl = slice(g * LANES, (g + 1) * LANES)
            q2 = jnp.concatenate([qn_ref[:, sl], qr_ref[:, sl]], axis=1)
            k2 = jnp.concatenate([kn_ref[:, sl], kr], axis=1)
            return _dot_nt(k2, q2)

        depth = 4
        ahead = [scores(g) for g in range(min(depth, group))]
        for g in range(group):
            sl = slice(g * LANES, (g + 1) * LANES)
            st = ahead.pop(0)
            if g + depth < group:
                ahead.append(scores(g + depth))
            if masked:
                st = jnp.where(adm, st, MASKED)
            m_old = m_ref[g:g + 1, :]
            m_new = jnp.maximum(m_old, jnp.max(st, axis=0, keepdims=True))
            alpha = jnp.exp2(m_old - m_new)
            pt = jnp.exp2(st - m_new)
            l_ref[g:g + 1, :] = alpha * l_ref[g:g + 1, :] + jnp.sum(pt, axis=0, keepdims=True)
            m_ref[g:g + 1, :] = m_new
            acc_ref[sl, :] = alpha * acc_ref[sl, :] + _dot_tn(v_ref[:, sl], pt.astype(BF16))

    reaches = (j + 1) * tk > i * tq

    @pl.when(reaches)
    def _():
        step(True)

    @pl.when(jnp.logical_not(reaches))
    def _():
        step(False)

    @pl.when(j == ((i + 1) * tq - 1) // tk)
    def _():
        for g in range(group):
            sl = slice(g * LANES, (g + 1) * LANES)
            o_ref[:, sl] = (acc_ref[sl, :] / l_ref[g:g + 1, :]).T.astype(o_ref.dtype)


def _mla_attention(qn, qr, kn, kr, vb, pos_row, pos_col, batch, seq, heads, tq, tk, group):
    nq = seq // tq
    nk = seq // tk
    gw = group * LANES
    pairs = [(i, j) for i in range(nq) for j in range(nk) if j * tk < (i + 1) * tq]
    it = jnp.asarray([p[0] for p in pairs], jnp.int32)
    jt = jnp.asarray([p[1] for p in pairs], jnp.int32)
    qmap = lambda b, h, p, it, jt: (b * nq + it[p], h)
    kmap = lambda b, h, p, it, jt: (b * nk + jt[p], h)
    grid_spec = pltpu.PrefetchScalarGridSpec(
        num_scalar_prefetch=2,
        grid=(batch, heads // group, len(pairs)),
        in_specs=[pl.BlockSpec((tq, gw), qmap),
                  pl.BlockSpec((tq, gw), qmap),
                  pl.BlockSpec((tk, gw), kmap),
                  pl.BlockSpec((tk, LANES), lambda b, h, p, it, jt: (b * nk + jt[p], 0)),
                  pl.BlockSpec((tk, gw), kmap),
                  pl.BlockSpec((1, 1, tq), lambda b, h, p, it, jt: (b, 0, it[p])),
                  pl.BlockSpec((tk, 1), lambda b, h, p, it, jt: (b * nk + jt[p], 0))],
        out_specs=pl.BlockSpec((tq, gw), qmap),
        scratch_shapes=[pltpu.VMEM((group, tq), F32), pltpu.VMEM((group, tq), F32),
                        pltpu.VMEM((gw, tq), F32)],
    )
    return pl.pallas_call(
        functools.partial(_mla_attn_kernel, group=group, tq=tq, tk=tk),
        grid_spec=grid_spec,
        out_shape=jax.ShapeDtypeStruct((batch * seq, heads * LANES), BF16),
        compiler_params=_params(("parallel", "parallel", "arbitrary")),
        name="mla_attn",
    )(it, jt, qn, qr, kn, kr, vb, pos_row, pos_col)


def _tree_sum8(x):
    parts = [x[r:r + 8] for r in range(0, x.shape[0], 8)]
    while len(parts) > 1:
        nxt = [parts[a] + parts[a + 1] for a in range(0, len(parts) - 1, 2)]
        if len(parts) % 2:
            nxt.append(parts[-1])
        parts = nxt
    return parts[0]


def _dsa_kernel(qi_ref, qa_ref, misc_ref, kidx_ref, ka_ref, va_ref, qpos_ref, kpos_ref,
                slopef_ref, o_ref,
                qis_ref, qas_ref, wt_ref, skey_ref, m_ref, l_ref, acc_ref,
                *, tq, kc, idx_heads, heads, topk, hg):
    i = pl.program_id(1)
    nkc = ((i + 1) * tq + kc - 1) // kc

    for h in range(idx_heads):
        qis_ref[h * tq:(h + 1) * tq, :] = qi_ref[:, h * LANES:(h + 1) * LANES]
    for h in range(heads):
        qas_ref[h * tq:(h + 1) * tq, :LANES] = qa_ref[:, h * LANES:(h + 1) * LANES]
    wt_ref[...] = misc_ref[...].T[QK_ROPE:QK_ROPE + idx_heads, :] * (
        (idx_heads ** -0.5) * (HEAD_DIM ** -0.5))
    qpos = qpos_ref[0]
    qchunk = qpos >> CHUNK_SHIFT

    def score_chunk(c, carry):
        r0 = pl.multiple_of(c * kc, kc)
        kx = kidx_ref[pl.ds(r0, kc), :].astype(BF16)
        score = jnp.zeros((kc, tq), F32)
        for g in range(idx_heads // hg):
            lt = _dot_nt(kx, qis_ref[g * hg * tq:(g + 1) * hg * tq, :])
            for hh in range(hg):
                h = g * hg + hh
                score += jnp.maximum(lt[:, hh * tq:(hh + 1) * tq], 0.0) * wt_ref[h:h + 1, :]
        adm = (kpos_ref[pl.ds(r0, kc), :] >> CHUNK_SHIFT) <= qchunk
        bits = pltpu.bitcast(jnp.where(adm, score, -jnp.inf), jnp.int32)
        skey_ref[pl.ds(r0, kc), :] = bits ^ ((bits >> 31) & 0x7FFFFFFF)
        return carry

    lax.fori_loop(0, nkc, score_chunk, 0)

    def count_ge(thr):
        def body(c, cnt):
            r0 = pl.multiple_of(c * kc, kc)
            hit = jnp.where(skey_ref[pl.ds(r0, kc), :] >= thr, 1.0, 0.0)
            return cnt + _tree_sum8(hit)
        cnt8 = lax.fori_loop(0, nkc, body, jnp.zeros((8, tq), F32))
        return jnp.sum(cnt8, axis=0, keepdims=True)

    need = float(topk)
    thr = jnp.where(count_ge(jnp.zeros((1, tq), jnp.int32)) >= need, 0, INT_MIN)

    def bit_step(it, thr):
        cand = thr + lax.shift_left(jnp.int32(1), 30 - it)
        return jnp.where(count_ge(cand) >= need, cand, thr)

    thr = lax.fori_loop(0, 31, bit_step, thr)
    thr = jnp.maximum(thr, NEG_INF_KEY + 1)

    @pl.when(jnp.max(count_ge(thr)) > need)
    def _():
        keep = need - count_ge(thr + 1)

        def ties_below(bound):
            def body(c, cnt):
                r0 = pl.multiple_of(c * kc, kc)
                idx = r0 + lax.broadcasted_iota(jnp.int32, (kc, 1), 0)
                tie = skey_ref[pl.ds(r0, kc), :] == thr
                hit = jnp.where(tie, jnp.where(idx < bound, 1.0, 0.0), 0.0)
                return cnt + _tree_sum8(hit)
            cnt8 = lax.fori_loop(0, nkc, body, jnp.zeros((8, tq), F32))
            return jnp.sum(cnt8, axis=0, keepdims=True)

        nbits = skey_ref.shape[0].bit_length()

        def bound_step(it, below):
            cand = below + lax.shift_left(jnp.int32(1), nbits - 1 - it)
            return jnp.where(ties_below(cand) < keep, cand, below)

        bound = lax.fori_loop(0, nbits, bound_step, jnp.zeros((1, tq), jnp.int32)) + 1

        def demote(c, carry):
            r0 = pl.multiple_of(c * kc, kc)
            idx = r0 + lax.broadcasted_iota(jnp.int32, (kc, 1), 0)
            k = skey_ref[pl.ds(r0, kc), :]
            skey_ref[pl.ds(r0, kc), :] = jnp.where(k == thr, jnp.where(idx >= bound, thr - 1, k), k)
            return carry

        lax.fori_loop(0, nkc, demote, 0)

    m_ref[...] = jnp.full(m_ref.shape, MASKED, F32)
    l_ref[...] = jnp.zeros(l_ref.shape, F32)
    acc_ref[...] = jnp.zeros(acc_ref.shape, F32)

    for h in range(heads):
        qas_ref[h * tq:(h + 1) * tq, LANES:] = jnp.broadcast_to(
            slopef_ref[h:h + 1, :], (tq, LANES)).astype(BF16)
    qpos0 = qpos[:, 0:1]
    lane = lax.broadcasted_iota(jnp.int32, (1, LANES), 1)

    def softmax_update(st, sel, vx, bias):
        m_all = m_ref[...]
        ps, ms, sums = [], [], []
        for h in range(heads):
            sl = slice(h * tq, (h + 1) * tq)
            s = st[:, sl] if bias is None else st[:, sl] - bias(h)
            s = jnp.where(sel, s, MASKED)
            m_new = jnp.maximum(m_all[:, sl], jnp.max(s, axis=0, keepdims=True))
            p = jnp.exp2(s - m_new)
            ms.append(m_new)
            sums.append(jnp.sum(p, axis=0, keepdims=True))
            ps.append(p.astype(BF16))
        m_new = jnp.concatenate(ms, axis=1)
        alpha = jnp.exp2(m_all - m_new)
        m_ref[...] = m_new
        l_ref[...] = alpha * l_ref[...] + jnp.concatenate(sums, axis=1)
        acc_ref[...] = alpha * acc_ref[...] + _dot_tn(vx, jnp.concatenate(ps, axis=1))

    def attend_chunk(c, carry):
        r0 = pl.multiple_of(c * kc, kc)
        rel = kpos_ref[pl.ds(r0, kc), :] - qpos0
        feat = jnp.where(lane < 3, (rel >> CHUNK_SHIFT).astype(F32),
                         jnp.where(lane < 6, (rel & (CHUNK - 1)).astype(F32), 0.0))
        kx = jnp.concatenate([ka_ref[pl.ds(r0, kc), :].astype(BF16), feat.astype(BF16)], axis=1)
        vx = va_ref[pl.ds(r0, kc), :].astype(BF16)
        sel = skey_ref[pl.ds(r0, kc), :] >= jnp.where(rel < 0, thr, jnp.int32(2 ** 31 - 1))
        softmax_update(_dot_nt(kx, qas_ref[...]), sel, vx, None)
        return carry

    lax.fori_loop(0, (i * tq + kc - 1) // kc, attend_chunk, 0)

    r0 = pl.multiple_of(i * tq, tq)
    kp = kpos_ref[pl.ds(r0, tq), :]
    shifted = (jnp.abs(kp - qpos) - (qpos - qpos0)).astype(F32)
    sel = skey_ref[pl.ds(r0, tq), :] >= thr
    st = _dot_nt(ka_ref[pl.ds(r0, tq), :].astype(BF16), qas_ref[:, :LANES])
    softmax_update(st, sel, va_ref[pl.ds(r0, tq), :].astype(BF16),
                   lambda h: (LOG2E * 2.0 ** (-8.0 * (h + 1) / heads)) * shifted)

    for h in range(heads):
        sl = slice(h * tq, (h + 1) * tq)
        o_ref[:, h * LANES:(h + 1) * LANES] = (acc_ref[:, sl] / l_ref[:, sl]).T.astype(o_ref.dtype)


def _alibi_query_features(heads):
    rows = np.zeros((heads, LANES), np.float32)
    for h in range(heads):
        rest = LOG2E * 2.0 ** (-8.0 * (h + 1) / heads)
        for k in range(3):
            piece = float(np.asarray(rest, np.float32).astype(BF16).astype(np.float32))
            rows[h, k] = CHUNK * piece
            rows[h, 3 + k] = piece
            rest -= piece
    return jnp.asarray(rows)


def _dsa(p1, p2, pos_row, pos_col, batch, seq, heads, idx_heads, topk, kv_off, kc):
    tq = LANES
    nq = seq // tq
    hw = heads * LANES
    iw = idx_heads * LANES
    assert iw % hw == 0
    kblk = kv_off // LANES
    misc_blk = p2.shape[1] // LANES - 1
    hg = min(8, idx_heads)
    kernel = functools.partial(_dsa_kernel, tq=tq, kc=kc, idx_heads=idx_heads, heads=heads,
                               topk=topk, hg=hg)
    return pl.pallas_call(
        kernel,
        grid=(batch, nq),
        in_specs=[pl.BlockSpec((tq, iw), lambda b, i: (b * nq + i, 0)),
                  pl.BlockSpec((tq, hw), lambda b, i: (b * nq + i, iw // hw)),
                  pl.BlockSpec((tq, LANES), lambda b, i: (b * nq + i, misc_blk)),
                  pl.BlockSpec((seq, LANES), lambda b, i: (b, kblk + 2)),
                  pl.BlockSpec((seq, LANES), lambda b, i: (b, kblk)),
                  pl.BlockSpec((seq, LANES), lambda b, i: (b, kblk + 1)),
                  pl.BlockSpec((1, 1, tq), lambda b, i: (b, 0, i)),
                  pl.BlockSpec((seq, 1), lambda b, i: (b, 0)),
                  pl.BlockSpec((heads, LANES), lambda b, i: (0, 0))],
        out_specs=pl.BlockSpec((tq, hw), lambda b, i: (b * nq + i, 0)),
        out_shape=jax.ShapeDtypeStruct((batch * seq, hw), BF16),
        scratch_shapes=[pltpu.VMEM((idx_heads * tq, LANES), BF16),
                        pltpu.VMEM((heads * tq, 2 * LANES), BF16),
                        pltpu.VMEM((idx_heads, tq), F32),
                        pltpu.VMEM((seq, tq), jnp.int32),
                        pltpu.VMEM((1, heads * tq), F32),
                        pltpu.VMEM((1, heads * tq), F32),
                        pltpu.VMEM((LANES, heads * tq), F32)],
        compiler_params=_params(("parallel", "arbitrary")),
        name="dsa",
    )(p1, p1, p2, p2, p2, p2, pos_row, pos_col, _alibi_query_features(heads))


def _tile(n, want):
    t = min(n, want)
    assert n % t == 0, (n, want)
    return t


def _block(x, c, positions, w_ada, b_ada, ln1_g, w_in, q_norm_g, kv_norm_g, w_uq, w_uk, w_uv,
           w_o, ln2_g, w_mlp_in, w_mlp_out, final_g, *, heads, idx_heads, b_heads):
    batch, seq, d = x.shape
    t = batch * seq
    q_lora = w_uq.shape[0]
    kv_lora = w_uk.shape[0]
    d_ff = w_mlp_in.shape[1]
    hw = heads * HEAD_DIM
    iw = idx_heads * HEAD_DIM
    bw = b_heads * HEAD_DIM
    topk = min(TOPK_MAX, seq // 4)
    assert (1 << CHUNK_SHIFT) == CHUNK and QK_ROPE + idx_heads <= LANES

    splits = (hw, HEAD_DIM, HEAD_DIM, iw, HEAD_DIM, idx_heads, q_lora, kv_lora, QK_ROPE)
    offs = [0]
    for s in splits:
        offs.append(offs[-1] + s)
    piece = lambda k, scale=1.0: (offs[k], splits[k], scale)
    pieces = (piece(3), piece(0, HEAD_DIM ** -0.5 * LOG2E), piece(6), piece(7), piece(1),
              piece(2), piece(4), piece(8), piece(5))
    packed_width = -(-offs[-1] // LANES) * LANES
    w_pack = _pack_weights(w_in, pieces, packed_width, _tile(d, 256))
    width1 = hw + iw
    kv_off = q_lora + kv_lora

    wq = w_uq.reshape(q_lora, b_heads, HEAD_DIM + QK_ROPE) * ((HEAD_DIM + QK_ROPE) ** -0.5 * LOG2E)
    wqn = wq[:, :, :HEAD_DIM].reshape(q_lora, bw).astype(BF16)
    wqr = jnp.pad(wq[:, :, HEAD_DIM:], ((0, 0), (0, 0), (0, LANES - QK_ROPE))
                  ).reshape(q_lora, b_heads * LANES).astype(BF16)
    wuk = w_uk.astype(BF16)
    wuv = w_uv.astype(BF16)
    wo = w_o.astype(BF16)
    w1 = w_mlp_in.astype(BF16)
    w2 = w_mlp_out.astype(BF16)

    lane = jnp.arange(LANES)
    inv_row = (ROPE_THETA ** (-(2.0 * (lane % (QK_ROPE // 2))).astype(F32) / QK_ROPE)
               ).reshape(1, LANES)
    pos_row = positions.reshape(batch, 1, seq)
    pos_col = positions.reshape(t, 1)
    x2d = x.reshape(t, d)

    mod = _adaln_mod(c, w_ada, b_ada, _tile(6 * d, 512))
    shift1, scale1, gate1, shift2, scale2, gate2 = jnp.split(mod, 6, axis=-1)

    tm_ln = _tile(seq, 256)
    h = _ln_mod(x2d, ln1_g, scale1, shift1, seq, tm_ln)
    p1, p2 = _inproj(h, w_pack, width1, _tile(t, 1024), _tile(math.gcd(width1, w_pack.shape[1] - width1), 1024))

    out_a = _dsa(p1, p2, pos_row, pos_col, batch, seq, heads, idx_heads, topk, kv_off,
                 _tile(seq, 512))

    qn, qr, kn, vb, kr = _mla_proj(p2, pos_col, inv_row, q_norm_g.reshape(1, q_lora),
                                   kv_norm_g.reshape(1, kv_lora), wqn, wqr, wuk, wuv,
                                   q_lora, kv_lora, b_heads, _tile(t, 512))
    out_b = _mla_attention(qn, qr, kn, kr, vb, pos_row, pos_col, batch, seq, b_heads,
                           _tile(seq, 512), _tile(seq, 512), min(b_heads, 16))

    tm = _tile(seq, 1024)
    x1 = _mix_resid(out_a, out_b, wo, x2d, gate1, seq, tm, _tile(d, 1024))
    h2 = _ln_mod(x1, ln2_g, scale2, shift2, seq, tm_ln)
    hid = _mm_relu2(h2, w1, tm, _tile(d_ff, 1024))
    x2 = _mm_resid(hid, w2, x1, gate2, seq, tm, _tile(d, 1024), _tile(d_ff, 2048), "mlp_out")
    return _ln_final(x2, final_g, tm_ln).reshape(batch, seq, d)


def kernel(x, c, positions, w_ada, b_ada, ln1_g, w_in, q_norm_g, kv_norm_g, w_uq, w_uk, w_uv,
           w_o, ln2_g, w_mlp_in, w_mlp_out, final_g):
    assert w_ada.shape[0] == 1, "single-layer block"
    return _block(x, c, positions, w_ada[0], b_ada[0], ln1_g[0], w_in[0], q_norm_g[0],
                  kv_norm_g[0], w_uq[0], w_uk[0], w_uv[0], w_o[0], ln2_g[0], w_mlp_in[0],
                  w_mlp_out[0], final_g, heads=A_HEADS, idx_heads=IDX_HEADS, b_heads=B_HEADS)
```

```python
import functools
import math

import jax
import jax.numpy as jnp
import numpy as np
from jax import lax
from jax.experimental import pallas as pl
from jax.experimental.pallas import tpu as pltpu

F32 = jnp.float32
BF16 = jnp.bfloat16

EPS = 1e-6
CHUNK = 64
CHUNK_SHIFT = 6
A_HEADS = 16
IDX_HEADS = 32
HEAD_DIM = 128
TOPK_MAX = 256
B_HEADS = 16
QK_ROPE = 64
ROPE_THETA = 10000.0

LANES = 128
LOG2E = math.log2(math.e)
MASKED = -1e30
INT_MIN = -(2 ** 31)
NEG_INF_KEY = -2139095041
VMEM_LIMIT = 56 * 1024 * 1024


def _params(semantics, vmem=VMEM_LIMIT):
    return pltpu.CompilerParams(dimension_semantics=semantics, vmem_limit_bytes=vmem)


def _dot_nt(a, b):
    return lax.dot_general(a, b, (((1,), (1,)), ((), ())), preferred_element_type=F32)


def _dot_tn(a, b):
    return lax.dot_general(a, b, (((0,), (0,)), ((), ())), preferred_element_type=F32)


def _mod_kernel(c_ref, w_ref, b_ref, o_ref):
    c = c_ref[...]
    s = c * (1.0 / (1.0 + jnp.exp(-c)))
    o_ref[...] = jnp.dot(s.astype(BF16), w_ref[...].astype(BF16),
                         preferred_element_type=F32) + b_ref[...]


def _adaln_mod(c, w_ada, b_ada, tn):
    b, d = c.shape
    n = w_ada.shape[1]
    rows = 8
    c_pad = jnp.zeros((rows, d), F32).at[:b].set(c)
    out = pl.pallas_call(
        _mod_kernel,
        grid=(n // tn,),
        in_specs=[pl.BlockSpec((rows, d), lambda j: (0, 0)),
                  pl.BlockSpec((d, tn), lambda j: (0, j)),
                  pl.BlockSpec((1, tn), lambda j: (0, j))],
        out_specs=pl.BlockSpec((rows, tn), lambda j: (0, j)),
        out_shape=jax.ShapeDtypeStruct((rows, n), F32),
        compiler_params=_params(("arbitrary",)),
        name="adaln_mod",
    )(c_pad, w_ada, b_ada.reshape(1, n))
    return out[:b]


def _ln_mod_kernel(x_ref, g_ref, sc_ref, sh_ref, o_ref):
    x = x_ref[...]
    y = x * lax.rsqrt(jnp.mean(x * x, axis=-1, keepdims=True) + EPS) * g_ref[...]
    o_ref[...] = (y * (1.0 + sc_ref[0]) + sh_ref[0]).astype(o_ref.dtype)


def _ln_kernel(x_ref, g_ref, o_ref):
    x = x_ref[...]
    o_ref[...] = (x * lax.rsqrt(jnp.mean(x * x, axis=-1, keepdims=True) + EPS)
                  * g_ref[...]).astype(o_ref.dtype)


def _ln_mod(x2d, g, scale, shift, seq, tm):
    t, d = x2d.shape
    nb = scale.shape[0]
    per = seq // tm
    return pl.pallas_call(
        _ln_mod_kernel,
        grid=(t // tm,),
        in_specs=[pl.BlockSpec((tm, d), lambda i: (i, 0)),
                  pl.BlockSpec((1, d), lambda i: (0, 0)),
                  pl.BlockSpec((1, 1, d), lambda i: (i // per, 0, 0)),
                  pl.BlockSpec((1, 1, d), lambda i: (i // per, 0, 0))],
        out_specs=pl.BlockSpec((tm, d), lambda i: (i, 0)),
        out_shape=jax.ShapeDtypeStruct((t, d), BF16),
        compiler_params=_params(("parallel",)),
        name="ln_mod",
    )(x2d, g.reshape(1, d), scale.reshape(nb, 1, d), shift.reshape(nb, 1, d))


def _ln_final(x2d, g, tm):
    t, d = x2d.shape
    return pl.pallas_call(
        _ln_kernel,
        grid=(t // tm,),
        in_specs=[pl.BlockSpec((tm, d), lambda i: (i, 0)),
                  pl.BlockSpec((1, d), lambda i: (0, 0))],
        out_specs=pl.BlockSpec((tm, d), lambda i: (i, 0)),
        out_shape=jax.ShapeDtypeStruct((t, d), F32),
        compiler_params=_params(("parallel",)),
        name="ln_final",
    )(x2d, g.reshape(1, d))


def _pack_kernel(w_ref, o_ref, *, pieces, ncols):
    dst = 0
    for src, width, scale in pieces:
        lo = (src // LANES) * LANES
        hi = min(-(-(src + width) // LANES) * LANES, ncols)
        v = w_ref[:, lo:hi][:, src - lo:src - lo + width]
        if scale != 1.0:
            v = v * scale
        o_ref[:, dst:dst + width] = v.astype(o_ref.dtype)
        dst += width
    if dst < o_ref.shape[1]:
        o_ref[:, dst:] = jnp.zeros((o_ref.shape[0], o_ref.shape[1] - dst), o_ref.dtype)


def _pack_weights(w, pieces, width, tr):
    d, ncols = w.shape
    return pl.pallas_call(
        functools.partial(_pack_kernel, pieces=pieces, ncols=ncols),
        grid=(d // tr,),
        in_specs=[pl.BlockSpec((tr, ncols), lambda i: (i, 0))],
        out_specs=pl.BlockSpec((tr, width), lambda i: (i, 0)),
        out_shape=jax.ShapeDtypeStruct((d, width), BF16),
        compiler_params=_params(("parallel",)),
        name="pack_w_in",
    )(w)


def _inproj_kernel(h_ref, w_ref, o1_ref, o2_ref, *, n1):
    j = pl.program_id(1)
    acc = lambda: jnp.dot(h_ref[...], w_ref[...], preferred_element_type=F32)

    @pl.when(j < n1)
    def _():
        o1_ref[...] = acc().astype(o1_ref.dtype)

    @pl.when(j >= n1)
    def _():
        o2_ref[...] = acc()


def _inproj(h, w, width1, tm, tn):
    t, d = h.shape
    n = w.shape[1]
    n1 = width1 // tn
    return pl.pallas_call(
        functools.partial(_inproj_kernel, n1=n1),
        grid=(t // tm, n // tn),
        in_specs=[pl.BlockSpec((tm, d), lambda i, j: (i, 0)),
                  pl.BlockSpec((d, tn), lambda i, j: (0, j))],
        out_specs=[pl.BlockSpec((tm, tn), lambda i, j: (i, jnp.minimum(j, n1 - 1))),
                   pl.BlockSpec((tm, tn), lambda i, j: (i, jnp.maximum(j - n1, 0)))],
        out_shape=[jax.ShapeDtypeStruct((t, width1), BF16),
                   jax.ShapeDtypeStruct((t, n - width1), F32)],
        compiler_params=_params(("parallel", "arbitrary")),
        name="in_proj",
    )(h, w)


def _mm_relu2_kernel(a_ref, w_ref, o_ref):
    acc = jnp.dot(a_ref[...], w_ref[...], preferred_element_type=F32)
    r = jnp.maximum(acc, 0.0)
    o_ref[...] = (r * r).astype(o_ref.dtype)


def _mm_relu2(a, w, tm, tn):
    t, d = a.shape
    n = w.shape[1]
    return pl.pallas_call(
        _mm_relu2_kernel,
        grid=(t // tm, n // tn),
        in_specs=[pl.BlockSpec((tm, d), lambda i, j: (i, 0)),
                  pl.BlockSpec((d, tn), lambda i, j: (0, j))],
        out_specs=pl.BlockSpec((tm, tn), lambda i, j: (i, j)),
        out_shape=jax.ShapeDtypeStruct((t, n), BF16),
        compiler_params=_params(("parallel", "arbitrary")),
        name="mlp_in",
    )(a, w)


def _mm_resid_kernel(a_ref, w_ref, r_ref, g_ref, o_ref, *, nk):
    k = pl.program_id(2)
    part = lambda: jnp.dot(a_ref[...], w_ref[...], preferred_element_type=F32)
    if nk == 1:
        o_ref[...] = r_ref[...] + g_ref[0] * part()
        return

    @pl.when(k == 0)
    def _():
        o_ref[...] = part()

    @pl.when((k > 0) & (k < nk - 1))
    def _():
        o_ref[...] += part()

    @pl.when(k == nk - 1)
    def _():
        o_ref[...] = r_ref[...] + g_ref[0] * (o_ref[...] + part())


def _mm_resid(a, w, resid, gate, seq, tm, tn, tk, name):
    t, kdim = a.shape
    n = w.shape[1]
    nb = gate.shape[0]
    per = seq // tm
    nk = kdim // tk
    return pl.pallas_call(
        functools.partial(_mm_resid_kernel, nk=nk),
        grid=(t // tm, n // tn, nk),
        in_specs=[pl.BlockSpec((tm, tk), lambda i, j, k: (i, k)),
                  pl.BlockSpec((tk, tn), lambda i, j, k: (k, j)),
                  pl.BlockSpec((tm, tn), lambda i, j, k: (i, j)),
                  pl.BlockSpec((1, 1, tn), lambda i, j, k: (i // per, 0, j))],
        out_specs=pl.BlockSpec((tm, tn), lambda i, j, k: (i, j)),
        out_shape=jax.ShapeDtypeStruct((t, n), F32),
        compiler_params=_params(("parallel", "parallel", "arbitrary")),
        name=name,
    )(a, w, resid, gate.reshape(nb, 1, n))


def _mix_resid_kernel(a1_ref, a2_ref, w1_ref, w2_ref, r_ref, g_ref, o_ref):
    acc = jnp.dot(a1_ref[...], w1_ref[...], preferred_element_type=F32)
    acc += jnp.dot(a2_ref[...], w2_ref[...], preferred_element_type=F32)
    o_ref[...] = r_ref[...] + g_ref[0] * acc


def _mix_resid(a1, a2, w, resid, gate, seq, tm, tn):
    t, k1 = a1.shape
    k2 = a2.shape[1]
    assert k1 == k2
    n = w.shape[1]
    nb = gate.shape[0]
    per = seq // tm
    return pl.pallas_call(
        _mix_resid_kernel,
        grid=(t // tm, n // tn),
        in_specs=[pl.BlockSpec((tm, k1), lambda i, j: (i, 0)),
                  pl.BlockSpec((tm, k2), lambda i, j: (i, 0)),
                  pl.BlockSpec((k1, tn), lambda i, j: (0, j)),
                  pl.BlockSpec((k2, tn), lambda i, j: (1, j)),
                  pl.BlockSpec((tm, tn), lambda i, j: (i, j)),
                  pl.BlockSpec((1, 1, tn), lambda i, j: (i // per, 0, j))],
        out_specs=pl.BlockSpec((tm, tn), lambda i, j: (i, j)),
        out_shape=jax.ShapeDtypeStruct((t, n), F32),
        compiler_params=_params(("parallel", "arbitrary")),
        name="out_proj",
    )(a1, a2, w, w, resid, gate.reshape(nb, 1, n))


def _rope_tables(pos_col, inv_row):
    ang = pos_col.astype(F32) * inv_row
    lane = lax.broadcasted_iota(jnp.int32, ang.shape, 1)
    half = QK_ROPE // 2
    cos = jnp.cos(ang)
    sin = jnp.sin(ang)
    c = jnp.where(lane < QK_ROPE, cos, 0.0)
    s_lo = jnp.where(lane < half, -sin, 0.0)
    s_hi = jnp.where((lane >= half) & (lane < QK_ROPE), sin, 0.0)
    return c, s_lo, s_hi


def _rope_apply(x, c, s_lo, s_hi):
    half = QK_ROPE // 2
    return (x * c + pltpu.roll(x, LANES - half, axis=1) * s_lo
            + pltpu.roll(x, half, axis=1) * s_hi)


def _mla_proj_kernel(cq_ref, ckv_ref, misc_ref, pos_ref, inv_ref, gq_ref, gkv_ref,
                     wqn_ref, wqr_ref, wuk_ref, wuv_ref,
                     qn_ref, qr_ref, kn_ref, vb_ref, kr_ref, *, heads):
    cq = cq_ref[...]
    cqn = (cq * lax.rsqrt(jnp.mean(cq * cq, axis=-1, keepdims=True) + EPS)
           * gq_ref[...]).astype(BF16)
    ckv = ckv_ref[...]
    kvn = (ckv * lax.rsqrt(jnp.mean(ckv * ckv, axis=-1, keepdims=True) + EPS)
           * gkv_ref[...]).astype(BF16)
    c, s_lo, s_hi = _rope_tables(pos_ref[...], inv_ref[...])

    qn_ref[...] = jnp.dot(cqn, wqn_ref[...], preferred_element_type=F32).astype(BF16)
    qr = jnp.dot(cqn, wqr_ref[...], preferred_element_type=F32)
    for h in range(heads):
        sl = slice(h * LANES, (h + 1) * LANES)
        qr_ref[:, sl] = _rope_apply(qr[:, sl], c, s_lo, s_hi).astype(BF16)
    kn_ref[...] = jnp.dot(kvn, wuk_ref[...], preferred_element_type=F32).astype(BF16)
    vb_ref[...] = jnp.dot(kvn, wuv_ref[...], preferred_element_type=F32).astype(BF16)
    kr_ref[...] = _rope_apply(misc_ref[...], c, s_lo, s_hi).astype(BF16)


def _mla_proj(p2, pos_col, inv_row, gq, gkv, wqn, wqr, wuk, wuv, q_lora, kv_lora, heads, tm):
    t = p2.shape[0]
    hw = heads * LANES
    misc_blk = p2.shape[1] // LANES - 1
    const = lambda i: (0, 0)
    return pl.pallas_call(
        functools.partial(_mla_proj_kernel, heads=heads),
        grid=(t // tm,),
        in_specs=[pl.BlockSpec((tm, q_lora), lambda i: (i, 0)),
                  pl.BlockSpec((tm, kv_lora), lambda i: (i, q_lora // kv_lora)),
                  pl.BlockSpec((tm, LANES), lambda i: (i, misc_blk)),
                  pl.BlockSpec((tm, 1), lambda i: (i, 0)),
                  pl.BlockSpec((1, LANES), const),
                  pl.BlockSpec((1, q_lora), const),
                  pl.BlockSpec((1, kv_lora), const),
                  pl.BlockSpec((q_lora, hw), const),
                  pl.BlockSpec((q_lora, hw), const),
                  pl.BlockSpec((kv_lora, hw), const),
                  pl.BlockSpec((kv_lora, hw), const)],
        out_specs=[pl.BlockSpec((tm, hw), lambda i: (i, 0))] * 4
                  + [pl.BlockSpec((tm, LANES), lambda i: (i, 0))],
        out_shape=[jax.ShapeDtypeStruct((t, hw), BF16)] * 4
                  + [jax.ShapeDtypeStruct((t, LANES), BF16)],
        compiler_params=_params(("parallel",)),
        name="mla_proj",
    )(p2, p2, p2, pos_col, inv_row, gq, gkv, wqn, wqr, wuk, wuv)


def _mla_attn_kernel(it_ref, jt_ref, qn_ref, qr_ref, kn_ref, kr_ref, v_ref, qpos_ref, kpos_ref,
                     o_ref, m_ref, l_ref, acc_ref, *, group, tq, tk):
    p = pl.program_id(2)
    i = it_ref[p]
    j = jt_ref[p]

    @pl.when(j == 0)
    def _():
        m_ref[...] = jnp.full(m_ref.shape, MASKED, F32)
        l_ref[...] = jnp.zeros(l_ref.shape, F32)
        acc_ref[...] = jnp.zeros(acc_ref.shape, F32)

    def step(masked):
        kr = kr_ref[...]
        if masked:
            adm = (kpos_ref[...] >> CHUNK_SHIFT) <= (qpos_ref[0] >> CHUNK_SHIFT)
        def scores(g):
            sl = slice(g * LANES, (g + 1) * LANES)
            q2 = jnp.concatenate([qn_ref[:, sl], qr_ref[:, sl]], axis=1)
            k2 = jnp.concatenate([kn_ref[:, sl], kr], axis=1)
            return _dot_nt(k2, q2)

        depth = 4
        ahead = [scores(g) for g in range(min(depth, group))]
        for g in range(group):
            sl = slice(g * LANES, (g + 1) * LANES)
            st = ahead.pop(0)
            if g + depth < group:
                ahead.append(scores(g + depth))
            if masked:
                st = jnp.where(adm, st, MASKED)
            m_old = m_ref[g:g + 1, :]
            m_new = jnp.maximum(m_old, jnp.max(st, axis=0, keepdims=True))
            alpha = jnp.exp2(m_old - m_new)
            pt = jnp.exp2(st - m_new)
            l_ref[g:g + 1, :] = alpha * l_ref[g:g + 1, :] + jnp.sum(pt, axis=0, keepdims=True)
            m_ref[g:g + 1, :] = m_new
            acc_ref[sl, :] = alpha * acc_ref[sl, :] + _dot_tn(v_ref[:, sl], pt.astype(BF16))

    reaches = (j + 1) * tk > i * tq

    @pl.when(reaches)
    def _():
        step(True)

    @pl.when(jnp.logical_not(reaches))
    def _():
        step(False)

    @pl.when(j == ((i + 1) * tq - 1) // tk)
    def _():
        for g in range(group):
            sl = slice(g * LANES, (g + 1) * LANES)
            o_ref[:, sl] = (acc_ref[sl, :] / l_ref[g:g + 1, :]).T.astype(o_ref.dtype)


def _mla_attention(qn, qr, kn, kr, vb, pos_row, pos_col, batch, seq, heads, tq, tk, group):
    nq = seq // tq
    nk = seq // tk
    gw = group * LANES
    pairs = [(i, j) for i in range(nq) for j in range(nk) if j * tk < (i + 1) * tq]
    it = jnp.asarray([p[0] for p in pairs], jnp.int32)
    jt = jnp.asarray([p[1] for p in pairs], jnp.int32)
    qmap = lambda b, h, p, it, jt: (b * nq + it[p], h)
    kmap = lambda b, h, p, it, jt: (b * nk + jt[p], h)
    grid_spec = pltpu.PrefetchScalarGridSpec(
        num_scalar_prefetch=2,
        grid=(batch, heads // group, len(pairs)),
        in_specs=[pl.BlockSpec((tq, gw), qmap),
                  pl.BlockSpec((tq, gw), qmap),
                  pl.BlockSpec((tk, gw), kmap),
                  pl.BlockSpec((tk, LANES), lambda b, h, p, it, jt: (b * nk + jt[p], 0)),
                  pl.BlockSpec((tk, gw), kmap),
                  pl.BlockSpec((1, 1, tq), lambda b, h, p, it, jt: (b, 0, it[p])),
                  pl.BlockSpec((tk, 1), lambda b, h, p, it, jt: (b * nk + jt[p], 0))],
        out_specs=pl.BlockSpec((tq, gw), qmap),
        scratch_shapes=[pltpu.VMEM((group, tq), F32), pltpu.VMEM((group, tq), F32),
                        pltpu.VMEM((gw, tq), F32)],
    )
    return pl.pallas_call(
        functools.partial(_mla_attn_kernel, group=group, tq=tq, tk=tk),
        grid_spec=grid_spec,
        out_shape=jax.ShapeDtypeStruct((batch * seq, heads * LANES), BF16),
        compiler_params=_params(("parallel", "parallel", "arbitrary")),
        name="mla_attn",
    )(it, jt, qn, qr, kn, kr, vb, pos_row, pos_col)


def _tree_sum8(x):
    parts = [x[r:r + 8] for r in range(0, x.shape[0], 8)]
    while len(parts) > 1:
        nxt = [parts[a] + parts[a + 1] for a in range(0, len(parts) - 1, 2)]
        if len(parts) % 2:
            nxt.append(parts[-1])
        parts = nxt
    return parts[0]


def _dsa_kernel(qi_ref, qa_ref, misc_ref, kidx_ref, ka_ref, va_ref, qpos_ref, kpos_ref,
                slopef_ref, o_ref,
                qis_ref, qas_ref, wt_ref, skey_ref, m_ref, l_ref, acc_ref,
                *, tq, kc, idx_heads, heads, topk, hg):
    i = pl.program_id(1)
    nkc = ((i + 1) * tq + kc - 1) // kc

    for h in range(idx_heads):
        qis_ref[h * tq:(h + 1) * tq, :] = qi_ref[:, h * LANES:(h + 1) * LANES]
    for h in range(heads):
        qas_ref[h * tq:(h + 1) * tq, :LANES] = qa_ref[:, h * LANES:(h + 1) * LANES]
    wt_ref[...] = misc_ref[...].T[QK_ROPE:QK_ROPE + idx_heads, :] * (
        (idx_heads ** -0.5) * (HEAD_DIM ** -0.5))
    qpos = qpos_ref[0]
    qchunk = qpos >> CHUNK_SHIFT

    def score_chunk(c, carry):
        r0 = pl.multiple_of(c * kc, kc)
        kx = kidx_ref[pl.ds(r0, kc), :].astype(BF16)
        score = jnp.zeros((kc, tq), F32)
        for g in range(idx_heads // hg):
            lt = _dot_nt(kx, qis_ref[g * hg * tq:(g + 1) * hg * tq, :])
            for hh in range(hg):
                h = g * hg + hh
                score += jnp.maximum(lt[:, hh * tq:(hh + 1) * tq], 0.0) * wt_ref[h:h + 1, :]
        adm = (kpos_ref[pl.ds(r0, kc), :] >> CHUNK_SHIFT) <= qchunk
        bits = pltpu.bitcast(jnp.where(adm, score, -jnp.inf), jnp.int32)
        skey_ref[pl.ds(r0, kc), :] = bits ^ ((bits >> 31) & 0x7FFFFFFF)
        return carry

    lax.fori_loop(0, nkc, score_chunk, 0)

    def count_ge(thr):
        def body(c, cnt):
            r0 = pl.multiple_of(c * kc, kc)
            hit = jnp.where(skey_ref[pl.ds(r0, kc), :] >= thr, 1.0, 0.0)
            return cnt + _tree_sum8(hit)
        cnt8 = lax.fori_loop(0, nkc, body, jnp.zeros((8, tq), F32))
        return jnp.sum(cnt8, axis=0, keepdims=True)

    need = float(topk)
    thr = jnp.where(count_ge(jnp.zeros((1, tq), jnp.int32)) >= need, 0, INT_MIN)

    def bit_step(it, thr):
        cand = thr + lax.shift_left(jnp.int32(1), 30 - it)
        return jnp.where(count_ge(cand) >= need, cand, thr)

    thr = lax.fori_loop(0, 31, bit_step, thr)
    thr = jnp.maximum(thr, NEG_INF_KEY + 1)

    @pl.when(jnp.max(count_ge(thr)) > need)
    def _():
        keep = need - count_ge(thr + 1)

        def ties_below(bound):
            def body(c, cnt):
                r0 = pl.multiple_of(c * kc, kc)
                idx = r0 + lax.broadcasted_iota(jnp.int32, (kc, 1), 0)
                tie = skey_ref[pl.ds(r0, kc), :] == thr
                hit = jnp.where(tie, jnp.where(idx < bound, 1.0, 0.0), 0.0)
                return cnt + _tree_sum8(hit)
            cnt8 = lax.fori_loop(0, nkc, body, jnp.zeros((8, tq), F32))
            return jnp.sum(cnt8, axis=0, keepdims=True)

        nbits = skey_ref.shape[0].bit_length()

        def bound_step(it, below):
            cand = below + lax.shift_left(jnp.int32(1), nbits - 1 - it)
            return jnp.where(ties_below(cand) < keep, cand, below)

        bound = lax.fori_loop(0, nbits, bound_step, jnp.zeros((1, tq), jnp.int32)) + 1

        def demote(c, carry):
            r0 = pl.multiple_of(c * kc, kc)
            idx = r0 + lax.broadcasted_iota(jnp.int32, (kc, 1), 0)
            k = skey_ref[pl.ds(r0, kc), :]
            skey_ref[pl.ds(r0, kc), :] = jnp.where(k == thr, jnp.where(idx >= bound, thr - 1, k), k)
            return carry

        lax.fori_loop(0, nkc, demote, 0)

    m_ref[...] = jnp.full(m_ref.shape, MASKED, F32)
    l_ref[...] = jnp.zeros(l_ref.shape, F32)
    acc_ref[...] = jnp.zeros(acc_ref.shape, F32)

    for h in range(heads):
        qas_ref[h * tq:(h + 1) * tq, LANES:] = jnp.broadcast_to(
            slopef_ref[h:h + 1, :], (tq, LANES)).astype(BF16)
    qpos0 = qpos[:, 0:1]
    lane = lax.broadcasted_iota(jnp.int32, (1, LANES), 1)

    def softmax_update(st, sel, vx, bias):
        m_all = m_ref[...]
        ps, ms, sums = [], [], []
        for h in range(heads):
            sl = slice(h * tq, (h + 1) * tq)
            s = st[:, sl] if bias is None else st[:, sl] - bias(h)
            s = jnp.where(sel, s, MASKED)
            m_new = jnp.maximum(m_all[:, sl], jnp.max(s, axis=0, keepdims=True))
            p = jnp.exp2(s - m_new)
            ms.append(m_new)
            sums.append(jnp.sum(p, axis=0, keepdims=True))
            ps.append(p.astype(BF16))
        m_new = jnp.concatenate(ms, axis=1)
        alpha = jnp.exp2(m_all - m_new)
        m_ref[...] = m_new
        l_ref[...] = alpha * l_ref[...] + jnp.concatenate(sums, axis=1)
        acc_ref[...] = alpha * acc_ref[...] + _dot_tn(vx, jnp.concatenate(ps, axis=1))

    def attend_chunk(c, carry):
        r0 = pl.multiple_of(c * kc, kc)
        rel = kpos_ref[pl.ds(r0, kc), :] - qpos0
        feat = jnp.where(lane < 3, (rel >> CHUNK_SHIFT).astype(F32),
                         jnp.where(lane < 6, (rel & (CHUNK - 1)).astype(F32), 0.0))
        kx = jnp.concatenate([ka_ref[pl.ds(r0, kc), :].astype(BF16), feat.astype(BF16)], axis=1)
        vx = va_ref[pl.ds(r0, kc), :].astype(BF16)
        sel = skey_ref[pl.ds(r0, kc), :] >= jnp.where(rel < 0, thr, jnp.int32(2 ** 31 - 1))
        softmax_update(_dot_nt(kx, qas_ref[...]), sel, vx, None)
        return carry

    lax.fori_loop(0, (i * tq + kc - 1) // kc, attend_chunk, 0)

    r0 = pl.multiple_of(i * tq, tq)
    kp = kpos_ref[pl.ds(r0, tq), :]
    shifted = (jnp.abs(kp - qpos) - (qpos - qpos0)).astype(F32)
    sel = skey_ref[pl.ds(r0, tq), :] >= thr
    st = _dot_nt(ka_ref[pl.ds(r0, tq), :].astype(BF16), qas_ref[:, :LANES])
    softmax_update(st, sel, va_ref[pl.ds(r0, tq), :].astype(BF16),
                   lambda h: (LOG2E * 2.0 ** (-8.0 * (h + 1) / heads)) * shifted)

    for h in range(heads):
        sl = slice(h * tq, (h + 1) * tq)
        o_ref[:, h * LANES:(h + 1) * LANES] = (acc_ref[:, sl] / l_ref[:, sl]).T.astype(o_ref.dtype)


def _alibi_query_features(heads):
    rows = np.zeros((heads, LANES), np.float32)
    for h in range(heads):
        rest = LOG2E * 2.0 ** (-8.0 * (h + 1) / heads)
        for k in range(3):
            piece = float(np.asarray(rest, np.float32).astype(BF16).astype(np.float32))
            rows[h, k] = CHUNK * piece
            rows[h, 3 + k] = piece
            rest -= piece
    return jnp.asarray(rows)


def _dsa(p1, p2, pos_row, pos_col, batch, seq, heads, idx_heads, topk, kv_off, kc):
    tq = LANES
    nq = seq // tq
    hw = heads * LANES
    iw = idx_heads * LANES
    assert iw % hw == 0
    kblk = kv_off // LANES
    misc_blk = p2.shape[1] // LANES - 1
    hg = min(8, idx_heads)
    kernel = functools.partial(_dsa_kernel, tq=tq, kc=kc, idx_heads=idx_heads, heads=heads,
                               topk=topk, hg=hg)
    return pl.pallas_call(
        kernel,
        grid=(batch, nq),
        in_specs=[pl.BlockSpec((tq, iw), lambda b, i: (b * nq + i, 0)),
                  pl.BlockSpec((tq, hw), lambda b, i: (b * nq + i, iw // hw)),
                  pl.BlockSpec((tq, LANES), lambda b, i: (b * nq + i, misc_blk)),
                  pl.BlockSpec((seq, LANES), lambda b, i: (b, kblk + 2)),
                  pl.BlockSpec((seq, LANES), lambda b, i: (b, kblk)),
                  pl.BlockSpec((seq, LANES), lambda b, i: (b, kblk + 1)),
                  pl.BlockSpec((1, 1, tq), lambda b, i: (b, 0, i)),
                  pl.BlockSpec((seq, 1), lambda b, i: (b, 0)),
                  pl.BlockSpec((heads, LANES), lambda b, i: (0, 0))],
        out_specs=pl.BlockSpec((tq, hw), lambda b, i: (b * nq + i, 0)),
        out_shape=jax.ShapeDtypeStruct((batch * seq, hw), BF16),
        scratch_shapes=[pltpu.VMEM((idx_heads * tq, LANES), BF16),
                        pltpu.VMEM((heads * tq, 2 * LANES), BF16),
                        pltpu.VMEM((idx_heads, tq), F32),
                        pltpu.VMEM((seq, tq), jnp.int32),
                        pltpu.VMEM((1, heads * tq), F32),
                        pltpu.VMEM((1, heads * tq), F32),
                        pltpu.VMEM((LANES, heads * tq), F32)],
        compiler_params=_params(("parallel", "arbitrary")),
        name="dsa",
    )(p1, p1, p2, p2, p2, p2, pos_row, pos_col, _alibi_query_features(heads))


def _tile(n, want):
    t = min(n, want)
    assert n % t == 0, (n, want)
    return t


def _block(x, c, positions, w_ada, b_ada, ln1_g, w_in, q_norm_g, kv_norm_g, w_uq, w_uk, w_uv,
           w_o, ln2_g, w_mlp_in, w_mlp_out, final_g, *, heads, idx_heads, b_heads):
    batch, seq, d = x.shape
    t = batch * seq
    q_lora = w_uq.shape[0]
    kv_lora = w_uk.shape[0]
    d_ff = w_mlp_in.shape[1]
    hw = heads * HEAD_DIM
    iw = idx_heads * HEAD_DIM
    bw = b_heads * HEAD_DIM
    topk = min(TOPK_MAX, seq // 4)
    assert (1 << CHUNK_SHIFT) == CHUNK and QK_ROPE + idx_heads <= LANES

    splits = (hw, HEAD_DIM, HEAD_DIM, iw, HEAD_DIM, idx_heads, q_lora, kv_lora, QK_ROPE)
    offs = [0]
    for s in splits:
        offs.append(offs[-1] + s)
    piece = lambda k: (offs[k], splits[k], 1.0)
    pieces = (piece(3), piece(0), piece(6), piece(7), piece(1), piece(2), piece(4), piece(8),
              piece(5))
    packed_width = -(-offs[-1] // LANES) * LANES
    col_scale = jnp.where(jnp.arange(offs[-1]) < hw, HEAD_DIM ** -0.5 * LOG2E, 1.0).astype(F32)
    w_pack = _pack_weights((w_in * col_scale).astype(BF16), pieces, packed_width, _tile(d, 256))
    width1 = hw + iw
    kv_off = q_lora + kv_lora

    wq = w_uq.reshape(q_lora, b_heads, HEAD_DIM + QK_ROPE) * ((HEAD_DIM + QK_ROPE) ** -0.5 * LOG2E)
    wqn = wq[:, :, :HEAD_DIM].reshape(q_lora, bw).astype(BF16)
    wqr = jnp.pad(wq[:, :, HEAD_DIM:], ((0, 0), (0, 0), (0, LANES - QK_ROPE))
                  ).reshape(q_lora, b_heads * LANES).astype(BF16)
    wuk = w_uk.astype(BF16)
    wuv = w_uv.astype(BF16)
    wo = w_o.astype(BF16)
    w1 = w_mlp_in.astype(BF16)
    w2 = w_mlp_out.astype(BF16)

    lane = jnp.arange(LANES)
    inv_row = (ROPE_THETA ** (-(2.0 * (lane % (QK_ROPE // 2))).astype(F32) / QK_ROPE)
               ).reshape(1, LANES)
    pos_row = positions.reshape(batch, 1, seq)
    pos_col = positions.reshape(t, 1)
    x2d = x.reshape(t, d)

    mod = _adaln_mod(c, w_ada, b_ada, _tile(6 * d, 512))
    shift1, scale1, gate1, shift2, scale2, gate2 = jnp.split(mod, 6, axis=-1)

    tm_ln = _tile(seq, 512)
    h = _ln_mod(x2d, ln1_g, scale1, shift1, seq, tm_ln)
    p1, p2 = _inproj(h, w_pack, width1, _tile(t, 1024), _tile(math.gcd(width1, w_pack.shape[1] - width1), 1024))

    out_a = _dsa(p1, p2, pos_row, pos_col, batch, seq, heads, idx_heads, topk, kv_off,
                 _tile(seq, 512))

    qn, qr, kn, vb, kr = _mla_proj(p2, pos_col, inv_row, q_norm_g.reshape(1, q_lora),
                                   kv_norm_g.reshape(1, kv_lora), wqn, wqr, wuk, wuv,
                                   q_lora, kv_lora, b_heads, _tile(t, 512))
    out_b = _mla_attention(qn, qr, kn, kr, vb, pos_row, pos_col, batch, seq, b_heads,
                           _tile(seq, 512), _tile(seq, 512), min(b_heads, 16))

    tm = _tile(seq, 1024)
    x1 = _mix_resid(out_a, out_b, wo, x2d, gate1, seq, tm, _tile(d, 1024))
    h2 = _ln_mod(x1, ln2_g, scale2, shift2, seq, tm_ln)
    hid = _mm_relu2(h2, w1, tm, _tile(d_ff, 1024))
    x2 = _mm_resid(hid, w2, x1, gate2, seq, tm, _tile(d, 2048), _tile(d_ff, 1024), "mlp_out")
    return _ln_final(x2, final_g, tm_ln).reshape(batch, seq, d)


def kernel(x, c, positions, w_ada, b_ada, ln1_g, w_in, q_norm_g, kv_norm_g, w_uq, w_uk, w_uv,
           w_o, ln2_g, w_mlp_in, w_mlp_out, final_g):
    assert w_ada.shape[0] == 1, "single-layer block"
    return _block(x, c, positions, w_ada[0], b_ada[0], ln1_g[0], w_in[0], q_norm_g[0],
                  kv_norm_g[0], w_uq[0], w_uk[0], w_uv[0], w_o[0], ln2_g[0], w_mlp_in[0],
                  w_mlp_out[0], final_g, heads=A_HEADS, idx_heads=IDX_HEADS, b_heads=B_HEADS)
```

```python
import functools
import math

import jax
import jax.numpy as jnp
import numpy as np
from jax import lax
from jax.experimental import pallas as pl
from jax.experimental.pallas import tpu as pltpu

F32 = jnp.float32
BF16 = jnp.bfloat16

EPS = 1e-6
CHUNK = 64
CHUNK_SHIFT = 6
A_HEADS = 16
IDX_HEADS = 32
HEAD_DIM = 128
TOPK_MAX = 256
B_HEADS = 16
QK_ROPE = 64
ROPE_THETA = 10000.0

LANES = 128
LOG2E = math.log2(math.e)
MASKED = -1e30
INT_MIN = -(2 ** 31)
NEG_INF_KEY = -2139095041
VMEM_LIMIT = 56 * 1024 * 1024
VMEM_LIMIT_WIDE = 60 * 1024 * 1024


def _params(semantics, vmem=VMEM_LIMIT):
    return pltpu.CompilerParams(dimension_semantics=semantics, vmem_limit_bytes=vmem)


def _dot_nt(a, b):
    return lax.dot_general(a, b, (((1,), (1,)), ((), ())), preferred_element_type=F32)


def _dot_tn(a, b):
    return lax.dot_general(a, b, (((0,), (0,)), ((), ())), preferred_element_type=F32)


def _mod_kernel(c_ref, w_ref, b_ref, o_ref):
    c = c_ref[...]
    s = c * (1.0 / (1.0 + jnp.exp(-c)))
    o_ref[...] = jnp.dot(s.astype(BF16), w_ref[...].astype(BF16),
                         preferred_element_type=F32) + b_ref[...]


def _adaln_mod(c, w_ada, b_ada, tn):
    b, d = c.shape
    n = w_ada.shape[1]
    rows = 8
    c_pad = jnp.zeros((rows, d), F32).at[:b].set(c)
    out = pl.pallas_call(
        _mod_kernel,
        grid=(n // tn,),
        in_specs=[pl.BlockSpec((rows, d), lambda j: (0, 0)),
                  pl.BlockSpec((d, tn), lambda j: (0, j)),
                  pl.BlockSpec((1, tn), lambda j: (0, j))],
        out_specs=pl.BlockSpec((rows, tn), lambda j: (0, j)),
        out_shape=jax.ShapeDtypeStruct((rows, n), F32),
        compiler_params=_params(("arbitrary",)),
        name="adaln_mod",
    )(c_pad, w_ada, b_ada.reshape(1, n))
    return out[:b]


def _ln_mod_kernel(x_ref, g_ref, sc_ref, sh_ref, o_ref):
    x = x_ref[...]
    y = x * lax.rsqrt(jnp.mean(x * x, axis=-1, keepdims=True) + EPS) * g_ref[...]
    o_ref[...] = (y * (1.0 + sc_ref[0]) + sh_ref[0]).astype(o_ref.dtype)


def _ln_mod(x2d, g, scale, shift, seq, tm):
    t, d = x2d.shape
    nb = scale.shape[0]
    per = seq // tm
    return pl.pallas_call(
        _ln_mod_kernel,
        grid=(t // tm,),
        in_specs=[pl.BlockSpec((tm, d), lambda i: (i, 0)),
                  pl.BlockSpec((1, d), lambda i: (0, 0)),
                  pl.BlockSpec((1, 1, d), lambda i: (i // per, 0, 0)),
                  pl.BlockSpec((1, 1, d), lambda i: (i // per, 0, 0))],
        out_specs=pl.BlockSpec((tm, d), lambda i: (i, 0)),
        out_shape=jax.ShapeDtypeStruct((t, d), BF16),
        compiler_params=_params(("parallel",)),
        name="ln_mod",
    )(x2d, g.reshape(1, d), scale.reshape(nb, 1, d), shift.reshape(nb, 1, d))


def _pack_kernel(w_ref, o_ref, *, pieces, ncols):
    dst = 0
    for src, width, scale in pieces:
        lo = (src // LANES) * LANES
        hi = min(-(-(src + width) // LANES) * LANES, ncols)
        v = w_ref[:, lo:hi][:, src - lo:src - lo + width]
        if scale != 1.0:
            v = v * scale
        o_ref[:, dst:dst + width] = v.astype(o_ref.dtype)
        dst += width
    if dst < o_ref.shape[1]:
        o_ref[:, dst:] = jnp.zeros((o_ref.shape[0], o_ref.shape[1] - dst), o_ref.dtype)


def _pack_weights(w, pieces, width, tr):
    d, ncols = w.shape
    return pl.pallas_call(
        functools.partial(_pack_kernel, pieces=pieces, ncols=ncols),
        grid=(d // tr,),
        in_specs=[pl.BlockSpec((tr, ncols), lambda i: (i, 0))],
        out_specs=pl.BlockSpec((tr, width), lambda i: (i, 0)),
        out_shape=jax.ShapeDtypeStruct((d, width), BF16),
        compiler_params=_params(("parallel",)),
        name="pack_w_in",
    )(w)


def _inproj_kernel(h_ref, w_ref, o1_ref, o2_ref, *, n1):
    j = pl.program_id(1)
    acc = lambda: jnp.dot(h_ref[...], w_ref[...], preferred_element_type=F32)

    @pl.when(j < n1)
    def _():
        o1_ref[...] = acc().astype(o1_ref.dtype)

    @pl.when(j >= n1)
    def _():
        o2_ref[...] = acc()


def _inproj(h, w, width1, tm, tn):
    t, d = h.shape
    n = w.shape[1]
    n1 = width1 // tn
    return pl.pallas_call(
        functools.partial(_inproj_kernel, n1=n1),
        grid=(t // tm, n // tn),
        in_specs=[pl.BlockSpec((tm, d), lambda i, j: (i, 0)),
                  pl.BlockSpec((d, tn), lambda i, j: (0, j))],
        out_specs=[pl.BlockSpec((tm, tn), lambda i, j: (i, jnp.minimum(j, n1 - 1))),
                   pl.BlockSpec((tm, tn), lambda i, j: (i, jnp.maximum(j - n1, 0)))],
        out_shape=[jax.ShapeDtypeStruct((t, width1), BF16),
                   jax.ShapeDtypeStruct((t, n - width1), F32)],
        compiler_params=_params(("parallel", "arbitrary")),
        name="in_proj",
    )(h, w)


def _mm_relu2_kernel(a_ref, w_ref, o_ref):
    acc = jnp.dot(a_ref[...], w_ref[...], preferred_element_type=F32)
    r = jnp.maximum(acc, 0.0)
    o_ref[...] = (r * r).astype(o_ref.dtype)


def _mm_relu2(a, w, tm, tn):
    t, d = a.shape
    n = w.shape[1]
    return pl.pallas_call(
        _mm_relu2_kernel,
        grid=(t // tm, n // tn),
        in_specs=[pl.BlockSpec((tm, d), lambda i, j: (i, 0)),
                  pl.BlockSpec((d, tn), lambda i, j: (0, j))],
        out_specs=pl.BlockSpec((tm, tn), lambda i, j: (i, j)),
        out_shape=jax.ShapeDtypeStruct((t, n), BF16),
        compiler_params=_params(("parallel", "arbitrary")),
        name="mlp_in",
    )(a, w)


def _mm_resid_kernel(a_ref, w_ref, r_ref, g_ref, fg_ref, o_ref, *, nk):
    k = pl.program_id(1)
    part = lambda: jnp.dot(a_ref[...], w_ref[...], preferred_element_type=F32)

    def finish(acc):
        y = r_ref[...] + g_ref[0] * acc
        o_ref[...] = y * lax.rsqrt(jnp.mean(y * y, axis=-1, keepdims=True) + EPS) * fg_ref[...]

    if nk == 1:
        finish(part())
        return

    @pl.when(k == 0)
    def _():
        o_ref[...] = part()

    @pl.when((k > 0) & (k < nk - 1))
    def _():
        o_ref[...] += part()

    @pl.when(k == nk - 1)
    def _():
        finish(o_ref[...] + part())


def _mm_resid_norm(a, w, resid, gate, norm_g, seq, tm, tk, name):
    t, kdim = a.shape
    n = w.shape[1]
    nb = gate.shape[0]
    per = seq // tm
    nk = kdim // tk
    return pl.pallas_call(
        functools.partial(_mm_resid_kernel, nk=nk),
        grid=(t // tm, nk),
        in_specs=[pl.BlockSpec((tm, tk), lambda i, k: (i, k)),
                  pl.BlockSpec((tk, n), lambda i, k: (k, 0)),
                  pl.BlockSpec((tm, n), lambda i, k: (i, 0)),
                  pl.BlockSpec((1, 1, n), lambda i, k: (i // per, 0, 0)),
                  pl.BlockSpec((1, n), lambda i, k: (0, 0))],
        out_specs=pl.BlockSpec((tm, n), lambda i, k: (i, 0)),
        out_shape=jax.ShapeDtypeStruct((t, n), F32),
        compiler_params=_params(("parallel", "arbitrary"), vmem=VMEM_LIMIT_WIDE),
        name=name,
    )(a, w, resid, gate.reshape(nb, 1, n), norm_g.reshape(1, n))


def _mix_resid_kernel(a1_ref, a2_ref, w1_ref, w2_ref, r_ref, g_ref, o_ref):
    acc = jnp.dot(a1_ref[...], w1_ref[...], preferred_element_type=F32)
    acc += jnp.dot(a2_ref[...], w2_ref[...], preferred_element_type=F32)
    o_ref[...] = r_ref[...] + g_ref[0] * acc


def _mix_resid(a1, a2, w, resid, gate, seq, tm, tn):
    t, k1 = a1.shape
    k2 = a2.shape[1]
    assert k1 == k2
    n = w.shape[1]
    nb = gate.shape[0]
    per = seq // tm
    return pl.pallas_call(
        _mix_resid_kernel,
        grid=(t // tm, n // tn),
        in_specs=[pl.BlockSpec((tm, k1), lambda i, j: (i, 0)),
                  pl.BlockSpec((tm, k2), lambda i, j: (i, 0)),
                  pl.BlockSpec((k1, tn), lambda i, j: (0, j)),
                  pl.BlockSpec((k2, tn), lambda i, j: (1, j)),
                  pl.BlockSpec((tm, tn), lambda i, j: (i, j)),
                  pl.BlockSpec((1, 1, tn), lambda i, j: (i // per, 0, j))],
        out_specs=pl.BlockSpec((tm, tn), lambda i, j: (i, j)),
        out_shape=jax.ShapeDtypeStruct((t, n), F32),
        compiler_params=_params(("parallel", "arbitrary")),
        name="out_proj",
    )(a1, a2, w, w, resid, gate.reshape(nb, 1, n))


def _rope_tables(pos_col, inv_row):
    ang = pos_col.astype(F32) * inv_row
    lane = lax.broadcasted_iota(jnp.int32, ang.shape, 1)
    half = QK_ROPE // 2
    cos = jnp.cos(ang)
    sin = jnp.sin(ang)
    c = jnp.where(lane < QK_ROPE, cos, 0.0)
    s_lo = jnp.where(lane < half, -sin, 0.0)
    s_hi = jnp.where((lane >= half) & (lane < QK_ROPE), sin, 0.0)
    return c, s_lo, s_hi


def _rope_apply(x, c, s_lo, s_hi):
    half = QK_ROPE // 2
    return (x * c + pltpu.roll(x, LANES - half, axis=1) * s_lo
            + pltpu.roll(x, half, axis=1) * s_hi)


def _mla_proj_kernel(cq_ref, ckv_ref, misc_ref, pos_ref, inv_ref, gq_ref, gkv_ref,
                     wqn_ref, wqr_ref, wuk_ref, wuv_ref,
                     qn_ref, qr_ref, kn_ref, vb_ref, kr_ref, *, heads):
    cq = cq_ref[...]
    cqn = (cq * lax.rsqrt(jnp.mean(cq * cq, axis=-1, keepdims=True) + EPS)
           * gq_ref[...]).astype(BF16)
    ckv = ckv_ref[...]
    kvn = (ckv * lax.rsqrt(jnp.mean(ckv * ckv, axis=-1, keepdims=True) + EPS)
           * gkv_ref[...]).astype(BF16)
    c, s_lo, s_hi = _rope_tables(pos_ref[...], inv_ref[...])

    qn_ref[...] = jnp.dot(cqn, wqn_ref[...], preferred_element_type=F32).astype(BF16)
    qr = jnp.dot(cqn, wqr_ref[...], preferred_element_type=F32)
    for h in range(heads):
        sl = slice(h * LANES, (h + 1) * LANES)
        qr_ref[:, sl] = _rope_apply(qr[:, sl], c, s_lo, s_hi).astype(BF16)
    kn_ref[...] = jnp.dot(kvn, wuk_ref[...], preferred_element_type=F32).astype(BF16)
    vb_ref[...] = jnp.dot(kvn, wuv_ref[...], preferred_element_type=F32).astype(BF16)
    kr_ref[...] = _rope_apply(misc_ref[...], c, s_lo, s_hi).astype(BF16)


def _mla_proj(p2, pos_col, inv_row, gq, gkv, wqn, wqr, wuk, wuv, q_lora, kv_lora, heads, tm):
    t = p2.shape[0]
    hw = heads * LANES
    misc_blk = p2.shape[1] // LANES - 1
    const = lambda i: (0, 0)
    return pl.pallas_call(
        functools.partial(_mla_proj_kernel, heads=heads),
        grid=(t // tm,),
        in_specs=[pl.BlockSpec((tm, q_lora), lambda i: (i, 0)),
                  pl.BlockSpec((tm, kv_lora), lambda i: (i, q_lora // kv_lora)),
                  pl.BlockSpec((tm, LANES), lambda i: (i, misc_blk)),
                  pl.BlockSpec((tm, 1), lambda i: (i, 0)),
                  pl.BlockSpec((1, LANES), const),
                  pl.BlockSpec((1, q_lora), const),
                  pl.BlockSpec((1, kv_lora), const),
                  pl.BlockSpec((q_lora, hw), const),
                  pl.BlockSpec((q_lora, hw), const),
                  pl.BlockSpec((kv_lora, hw), const),
                  pl.BlockSpec((kv_lora, hw), const)],
        out_specs=[pl.BlockSpec((tm, hw), lambda i: (i, 0))] * 4
                  + [pl.BlockSpec((tm, LANES), lambda i: (i, 0))],
        out_shape=[jax.ShapeDtypeStruct((t, hw), BF16)] * 4
                  + [jax.ShapeDtypeStruct((t, LANES), BF16)],
        compiler_params=_params(("parallel",)),
        name="mla_proj",
    )(p2, p2, p2, pos_col, inv_row, gq, gkv, wqn, wqr, wuk, wuv)


def _mla_attn_kernel(it_ref, jt_ref, qn_ref, qr_ref, kn_ref, kr_ref, v_ref, qpos_ref, kpos_ref,
                     o_ref, m_ref, l_ref, acc_ref, *, group, tq, tk):
    p = pl.program_id(2)
    i = it_ref[p]
    j = jt_ref[p]

    @pl.when(j == 0)
    def _():
        m_ref[...] = jnp.full(m_ref.shape, MASKED, F32)
        l_ref[...] = jnp.zeros(l_ref.shape, F32)
        acc_ref[...] = jnp.zeros(acc_ref.shape, F32)

    def step(masked):
        kr = kr_ref[...]
        if masked:
            adm = (kpos_ref[...] >> CHUNK_SHIFT) <= (qpos_ref[0] >> CHUNK_SHIFT)
        def scores(g):
            sl = slice(g * LANES, (g + 1) * LANES)
            q2 = jnp.concatenate([qn_ref[:, sl], qr_ref[:, sl]], axis=1)
            k2 = jnp.concatenate([kn_ref[:, sl], kr], axis=1)
            return _dot_nt(k2, q2)

        depth = 4
        ahead = [scores(g) for g in range(min(depth, group))]
        for g in range(group):
            sl = slice(g * LANES, (g + 1) * LANES)
            st = ahead.pop(0)
            if g + depth < group:
                ahead.append(scores(g + depth))
            if masked:
                st = jnp.where(adm, st, MASKED)
            m_old = m_ref[g:g + 1, :]
            m_new = jnp.maximum(m_old, jnp.max(st, axis=0, keepdims=True))
            alpha = jnp.exp2(m_old - m_new)
            pt = jnp.exp2(st - m_new)
            l_ref[g:g + 1, :] = alpha * l_ref[g:g + 1, :] + jnp.sum(pt, axis=0, keepdims=True)
            m_ref[g:g + 1, :] = m_new
            acc_ref[sl, :] = alpha * acc_ref[sl, :] + _dot_tn(v_ref[:, sl], pt.astype(BF16))

    reaches = (j + 1) * tk > i * tq

    @pl.when(reaches)
    def _():
        step(True)

    @pl.when(jnp.logical_not(reaches))
    def _():
        step(False)

    @pl.when(j == ((i + 1) * tq - 1) // tk)
    def _():
        for g in range(group):
            sl = slice(g * LANES, (g + 1) * LANES)
            o_ref[:, sl] = (acc_ref[sl, :] / l_ref[g:g + 1, :]).T.astype(o_ref.dtype)


def _mla_attention(qn, qr, kn, kr, vb, pos_row, pos_col, batch, seq, heads, tq, tk, group):
    nq = seq // tq
    nk = seq // tk
    gw = group * LANES
    pairs = [(i, j) for i in range(nq) for j in range(nk) if j * tk < (i + 1) * tq]
    it = jnp.asarray([p[0] for p in pairs], jnp.int32)
    jt = jnp.asarray([p[1] for p in pairs], jnp.int32)
    qmap = lambda b, h, p, it, jt: (b * nq + it[p], h)
    kmap = lambda b, h, p, it, jt: (b * nk + jt[p], h)
    grid_spec = pltpu.PrefetchScalarGridSpec(
        num_scalar_prefetch=2,
        grid=(batch, heads // group, len(pairs)),
        in_specs=[pl.BlockSpec((tq, gw), qmap),
                  pl.BlockSpec((tq, gw), qmap),
                  pl.BlockSpec((tk, gw), kmap),
                  pl.BlockSpec((tk, LANES), lambda b, h, p, it, jt: (b * nk + jt[p], 0)),
                  pl.BlockSpec((tk, gw), kmap),
                  pl.BlockSpec((1, 1, tq), lambda b, h, p, it, jt: (b, 0, it[p])),
                  pl.BlockSpec((tk, 1), lambda b, h, p, it, jt: (b * nk + jt[p], 0))],
        out_specs=pl.BlockSpec((tq, gw), qmap),
        scratch_shapes=[pltpu.VMEM((group, tq), F32), pltpu.VMEM((group, tq), F32),
                        pltpu.VMEM((gw, tq), F32)],
    )
    return pl.pallas_call(
        functools.partial(_mla_attn_kernel, group=group, tq=tq, tk=tk),
        grid_spec=grid_spec,
        out_shape=jax.ShapeDtypeStruct((batch * seq, heads * LANES), BF16),
        compiler_params=_params(("parallel", "parallel", "arbitrary")),
        name="mla_attn",
    )(it, jt, qn, qr, kn, kr, vb, pos_row, pos_col)


def _tree_sum8(x):
    parts = [x[r:r + 8] for r in range(0, x.shape[0], 8)]
    while len(parts) > 1:
        nxt = [parts[a] + parts[a + 1] for a in range(0, len(parts) - 1, 2)]
        if len(parts) % 2:
            nxt.append(parts[-1])
        parts = nxt
    return parts[0]


def _dsa_kernel(qi_ref, qa_ref, misc_ref, kidx_ref, ka_ref, va_ref, qpos_ref, kpos_ref,
                slopef_ref, o_ref,
                qis_ref, qas_ref, wt_ref, skey_ref, m_ref, l_ref, acc_ref,
                *, tq, kc, idx_heads, heads, topk, hg):
    i = pl.program_id(1)
    nkc = ((i + 1) * tq + kc - 1) // kc

    for h in range(idx_heads):
        qis_ref[h * tq:(h + 1) * tq, :] = qi_ref[:, h * LANES:(h + 1) * LANES]
    for h in range(heads):
        qas_ref[h * tq:(h + 1) * tq, :LANES] = qa_ref[:, h * LANES:(h + 1) * LANES]
    wt_ref[...] = misc_ref[...].T[QK_ROPE:QK_ROPE + idx_heads, :] * (
        (idx_heads ** -0.5) * (HEAD_DIM ** -0.5))
    qpos = qpos_ref[0]
    qchunk = qpos >> CHUNK_SHIFT

    def score_chunk(c, carry):
        r0 = pl.multiple_of(c * kc, kc)
        kx = kidx_ref[pl.ds(r0, kc), :].astype(BF16)
        score = jnp.zeros((kc, tq), F32)
        for g in range(idx_heads // hg):
            lt = _dot_nt(kx, qis_ref[g * hg * tq:(g + 1) * hg * tq, :])
            for hh in range(hg):
                h = g * hg + hh
                score += jnp.maximum(lt[:, hh * tq:(hh + 1) * tq], 0.0) * wt_ref[h:h + 1, :]
        adm = (kpos_ref[pl.ds(r0, kc), :] >> CHUNK_SHIFT) <= qchunk
        bits = pltpu.bitcast(jnp.where(adm, score, -jnp.inf), jnp.int32)
        skey_ref[pl.ds(r0, kc), :] = bits ^ ((bits >> 31) & 0x7FFFFFFF)
        return carry

    lax.fori_loop(0, nkc, score_chunk, 0)

    def count_ge(thr):
        def body(c, cnt):
            r0 = pl.multiple_of(c * kc, kc)
            hit = jnp.where(skey_ref[pl.ds(r0, kc), :] >= thr, 1.0, 0.0)
            return cnt + _tree_sum8(hit)
        cnt8 = lax.fori_loop(0, nkc, body, jnp.zeros((8, tq), F32))
        return jnp.sum(cnt8, axis=0, keepdims=True)

    need = float(topk)
    thr = jnp.where(count_ge(jnp.zeros((1, tq), jnp.int32)) >= need, 0, INT_MIN)

    def bit_step(it, thr):
        cand = thr + lax.shift_left(jnp.int32(1), 30 - it)
        return jnp.where(count_ge(cand) >= need, cand, thr)

    thr = lax.fori_loop(0, 31, bit_step, thr)
    thr = jnp.maximum(thr, NEG_INF_KEY + 1)

    @pl.when(jnp.max(count_ge(thr)) > need)
    def _():
        keep = need - count_ge(thr + 1)

        def ties_below(bound):
            def body(c, cnt):
                r0 = pl.multiple_of(c * kc, kc)
                idx = r0 + lax.broadcasted_iota(jnp.int32, (kc, 1), 0)
                tie = skey_ref[pl.ds(r0, kc), :] == thr
                hit = jnp.where(tie, jnp.where(idx < bound, 1.0, 0.0), 0.0)
                return cnt + _tree_sum8(hit)
            cnt8 = lax.fori_loop(0, nkc, body, jnp.zeros((8, tq), F32))
            return jnp.sum(cnt8, axis=0, keepdims=True)

        nbits = skey_ref.shape[0].bit_length()

        def bound_step(it, below):
            cand = below + lax.shift_left(jnp.int32(1), nbits - 1 - it)
            return jnp.where(ties_below(cand) < keep, cand, below)

        bound = lax.fori_loop(0, nbits, bound_step, jnp.zeros((1, tq), jnp.int32)) + 1

        def demote(c, carry):
            r0 = pl.multiple_of(c * kc, kc)
            idx = r0 + lax.broadcasted_iota(jnp.int32, (kc, 1), 0)
            k = skey_ref[pl.ds(r0, kc), :]
            skey_ref[pl.ds(r0, kc), :] = jnp.where(k == thr, jnp.where(idx >= bound, thr - 1, k), k)
            return carry

        lax.fori_loop(0, nkc, demote, 0)

    m_ref[...] = jnp.full(m_ref.shape, MASKED, F32)
    l_ref[...] = jnp.zeros(l_ref.shape, F32)
    acc_ref[...] = jnp.zeros(acc_ref.shape, F32)

    for h in range(heads):
        qas_ref[h * tq:(h + 1) * tq, LANES:] = jnp.broadcast_to(
            slopef_ref[h:h + 1, :], (tq, LANES)).astype(BF16)
    qpos0 = qpos[:, 0:1]
    lane = lax.broadcasted_iota(jnp.int32, (1, LANES), 1)

    def softmax_update(st, sel, vx, bias):
        m_all = m_ref[...]
        ps, ms, sums = [], [], []
        for h in range(heads):
            sl = slice(h * tq, (h + 1) * tq)
            s = st[:, sl] if bias is None else st[:, sl] - bias(h)
            s = jnp.where(sel, s, MASKED)
            m_new = jnp.maximum(m_all[:, sl], jnp.max(s, axis=0, keepdims=True))
            p = jnp.exp2(s - m_new)
            ms.append(m_new)
            sums.append(jnp.sum(p, axis=0, keepdims=True))
            ps.append(p.astype(BF16))
        m_new = jnp.concatenate(ms, axis=1)
        alpha = jnp.exp2(m_all - m_new)
        m_ref[...] = m_new
        l_ref[...] = alpha * l_ref[...] + jnp.concatenate(sums, axis=1)
        acc_ref[...] = alpha * acc_ref[...] + _dot_tn(vx, jnp.concatenate(ps, axis=1))

    def attend_chunk(c, carry):
        r0 = pl.multiple_of(c * kc, kc)
        rel = kpos_ref[pl.ds(r0, kc), :] - qpos0
        feat = jnp.where(lane < 3, (rel >> CHUNK_SHIFT).astype(F32),
                         jnp.where(lane < 6, (rel & (CHUNK - 1)).astype(F32), 0.0))
        kx = jnp.concatenate([ka_ref[pl.ds(r0, kc), :].astype(BF16), feat.astype(BF16)], axis=1)
        vx = va_ref[pl.ds(r0, kc), :].astype(BF16)
        sel = skey_ref[pl.ds(r0, kc), :] >= jnp.where(rel < 0, thr, jnp.int32(2 ** 31 - 1))
        softmax_update(_dot_nt(kx, qas_ref[...]), sel, vx, None)
        return carry

    lax.fori_loop(0, (i * tq + kc - 1) // kc, attend_chunk, 0)

    r0 = pl.multiple_of(i * tq, tq)
    kp = kpos_ref[pl.ds(r0, tq), :]
    shifted = (jnp.abs(kp - qpos) - (qpos - qpos0)).astype(F32)
    sel = skey_ref[pl.ds(r0, tq), :] >= thr
    st = _dot_nt(ka_ref[pl.ds(r0, tq), :].astype(BF16), qas_ref[:, :LANES])
    softmax_update(st, sel, va_ref[pl.ds(r0, tq), :].astype(BF16),
                   lambda h: (LOG2E * 2.0 ** (-8.0 * (h + 1) / heads)) * shifted)

    for h in range(heads):
        sl = slice(h * tq, (h + 1) * tq)
        o_ref[:, h * LANES:(h + 1) * LANES] = (acc_ref[:, sl] / l_ref[:, sl]).T.astype(o_ref.dtype)


def _alibi_query_features(heads):
    rows = np.zeros((heads, LANES), np.float32)
    for h in range(heads):
        rest = LOG2E * 2.0 ** (-8.0 * (h + 1) / heads)
        for k in range(3):
            piece = float(np.asarray(rest, np.float32).astype(BF16).astype(np.float32))
            rows[h, k] = CHUNK * piece
            rows[h, 3 + k] = piece
            rest -= piece
    return jnp.asarray(rows)


def _dsa(p1, p2, pos_row, pos_col, batch, seq, heads, idx_heads, topk, kv_off, kc):
    tq = LANES
    nq = seq // tq
    hw = heads * LANES
    iw = idx_heads * LANES
    assert iw % hw == 0
    kblk = kv_off // LANES
    misc_blk = p2.shape[1] // LANES - 1
    hg = min(8, idx_heads)
    kernel = functools.partial(_dsa_kernel, tq=tq, kc=kc, idx_heads=idx_heads, heads=heads,
                               topk=topk, hg=hg)
    return pl.pallas_call(
        kernel,
        grid=(batch, nq),
        in_specs=[pl.BlockSpec((tq, iw), lambda b, i: (b * nq + i, 0)),
                  pl.BlockSpec((tq, hw), lambda b, i: (b * nq + i, iw // hw)),
                  pl.BlockSpec((tq, LANES), lambda b, i: (b * nq + i, misc_blk)),
                  pl.BlockSpec((seq, LANES), lambda b, i: (b, kblk + 2)),
                  pl.BlockSpec((seq, LANES), lambda b, i: (b, kblk)),
                  pl.BlockSpec((seq, LANES), lambda b, i: (b, kblk + 1)),
                  pl.BlockSpec((1, 1, tq), lambda b, i: (b, 0, i)),
                  pl.BlockSpec((seq, 1), lambda b, i: (b, 0)),
                  pl.BlockSpec((heads, LANES), lambda b, i: (0, 0))],
        out_specs=pl.BlockSpec((tq, hw), lambda b, i: (b * nq + i, 0)),
        out_shape=jax.ShapeDtypeStruct((batch * seq, hw), BF16),
        scratch_shapes=[pltpu.VMEM((idx_heads * tq, LANES), BF16),
                        pltpu.VMEM((heads * tq, 2 * LANES), BF16),
                        pltpu.VMEM((idx_heads, tq), F32),
                        pltpu.VMEM((seq, tq), jnp.int32),
                        pltpu.VMEM((1, heads * tq), F32),
                        pltpu.VMEM((1, heads * tq), F32),
                        pltpu.VMEM((LANES, heads * tq), F32)],
        compiler_params=_params(("parallel", "arbitrary")),
        name="dsa",
    )(p1, p1, p2, p2, p2, p2, pos_row, pos_col, _alibi_query_features(heads))


def _tile(n, want):
    t = min(n, want)
    assert n % t == 0, (n, want)
    return t


def _block(x, c, positions, w_ada, b_ada, ln1_g, w_in, q_norm_g, kv_norm_g, w_uq, w_uk, w_uv,
           w_o, ln2_g, w_mlp_in, w_mlp_out, final_g, *, heads, idx_heads, b_heads):
    batch, seq, d = x.shape
    t = batch * seq
    q_lora = w_uq.shape[0]
    kv_lora = w_uk.shape[0]
    d_ff = w_mlp_in.shape[1]
    hw = heads * HEAD_DIM
    iw = idx_heads * HEAD_DIM
    bw = b_heads * HEAD_DIM
    topk = min(TOPK_MAX, seq // 4)
    assert (1 << CHUNK_SHIFT) == CHUNK and QK_ROPE + idx_heads <= LANES

    splits = (hw, HEAD_DIM, HEAD_DIM, iw, HEAD_DIM, idx_heads, q_lora, kv_lora, QK_ROPE)
    offs = [0]
    for s in splits:
        offs.append(offs[-1] + s)
    piece = lambda k, scale=1.0: (offs[k], splits[k], scale)
    pieces = (piece(3), piece(0, HEAD_DIM ** -0.5 * LOG2E), piece(6), piece(7), piece(1),
              piece(2), piece(4), piece(8), piece(5))
    packed_width = -(-offs[-1] // LANES) * LANES
    w_pack = _pack_weights(w_in, pieces, packed_width, _tile(d, 256))
    width1 = hw + iw
    kv_off = q_lora + kv_lora

    wq = w_uq.reshape(q_lora, b_heads, HEAD_DIM + QK_ROPE) * ((HEAD_DIM + QK_ROPE) ** -0.5 * LOG2E)
    wqn = wq[:, :, :HEAD_DIM].reshape(q_lora, bw).astype(BF16)
    wqr = jnp.pad(wq[:, :, HEAD_DIM:], ((0, 0), (0, 0), (0, LANES - QK_ROPE))
                  ).reshape(q_lora, b_heads * LANES).astype(BF16)
    wuk = w_uk.astype(BF16)
    wuv = w_uv.astype(BF16)
    wo = w_o.astype(BF16)
    w1 = w_mlp_in.astype(BF16)
    w2 = w_mlp_out.astype(BF16)

    lane = jnp.arange(LANES)
    inv_row = (ROPE_THETA ** (-(2.0 * (lane % (QK_ROPE // 2))).astype(F32) / QK_ROPE)
               ).reshape(1, LANES)
    pos_row = positions.reshape(batch, 1, seq)
    pos_col = positions.reshape(t, 1)
    x2d = x.reshape(t, d)

    mod = _adaln_mod(c, w_ada, b_ada, _tile(6 * d, 512))
    shift1, scale1, gate1, shift2, scale2, gate2 = jnp.split(mod, 6, axis=-1)

    tm_ln = _tile(seq, 512)
    h = _ln_mod(x2d, ln1_g, scale1, shift1, seq, tm_ln)
    p1, p2 = _inproj(h, w_pack, width1, _tile(t, 1024), _tile(math.gcd(width1, w_pack.shape[1] - width1), 1024))

    out_a = _dsa(p1, p2, pos_row, pos_col, batch, seq, heads, idx_heads, topk, kv_off,
                 _tile(seq, 512))

    qn, qr, kn, vb, kr = _mla_proj(p2, pos_col, inv_row, q_norm_g.reshape(1, q_lora),
                                   kv_norm_g.reshape(1, kv_lora), wqn, wqr, wuk, wuv,
                                   q_lora, kv_lora, b_heads, _tile(t, 512))
    out_b = _mla_attention(qn, qr, kn, kr, vb, pos_row, pos_col, batch, seq, b_heads,
                           _tile(seq, 512), _tile(seq, 512), min(b_heads, 16))

    tm = _tile(seq, 1024)
    x1 = _mix_resid(out_a, out_b, wo, x2d, gate1, seq, tm, _tile(d, 1024))
    h2 = _ln_mod(x1, ln2_g, scale2, shift2, seq, tm_ln)
    hid = _mm_relu2(h2, w1, tm, _tile(d_ff, 1024))
    out = _mm_resid_norm(hid, w2, x1, gate2, final_g, seq, _tile(seq, 512), _tile(d_ff, 1024),
                         "mlp_out")
    return out.reshape(batch, seq, d)


def kernel(x, c, positions, w_ada, b_ada, ln1_g, w_in, q_norm_g, kv_norm_g, w_uq, w_uk, w_uv,
           w_o, ln2_g, w_mlp_in, w_mlp_out, final_g):
    assert w_ada.shape[0] == 1, "single-layer block"
    return _block(x, c, positions, w_ada[0], b_ada[0], ln1_g[0], w_in[0], q_norm_g[0],
                  kv_norm_g[0], w_uq[0], w_uk[0], w_uv[0], w_o[0], ln2_g[0], w_mlp_in[0],
                  w_mlp_out[0], final_g, heads=A_HEADS, idx_heads=IDX_HEADS, b_heads=B_HEADS)
```

```python
import functools
import math

import jax
import jax.numpy as jnp
import numpy as np
from jax import lax
from jax.experimental import pallas as pl
from jax.experimental.pallas import tpu as pltpu

F32 = jnp.float32
BF16 = jnp.bfloat16

EPS = 1e-6
CHUNK = 64
CHUNK_SHIFT = 6
A_HEADS = 16
IDX_HEADS = 32
HEAD_DIM = 128
TOPK_MAX = 256
B_HEADS = 16
QK_ROPE = 64
ROPE_THETA = 10000.0

LANES = 128
LOG2E = math.log2(math.e)
MASKED = -1e30
INT_MIN = -(2 ** 31)
NEG_INF_KEY = -2139095041
VMEM_LIMIT = 56 * 1024 * 1024
VMEM_LIMIT_WIDE = 60 * 1024 * 1024


def _params(semantics, vmem=VMEM_LIMIT):
    return pltpu.CompilerParams(dimension_semantics=semantics, vmem_limit_bytes=vmem)


def _dot_nt(a, b):
    return lax.dot_general(a, b, (((1,), (1,)), ((), ())), preferred_element_type=F32)


def _dot_tn(a, b):
    return lax.dot_general(a, b, (((0,), (0,)), ((), ())), preferred_element_type=F32)


def _mod_kernel(c_ref, w_ref, b_ref, o_ref):
    c = c_ref[...]
    s = c * (1.0 / (1.0 + jnp.exp(-c)))
    o_ref[...] = jnp.dot(s.astype(BF16), w_ref[...].astype(BF16),
                         preferred_element_type=F32) + b_ref[...]


def _adaln_mod(c, w_ada, b_ada, tn):
    b, d = c.shape
    n = w_ada.shape[1]
    rows = 8
    c_pad = jnp.zeros((rows, d), F32).at[:b].set(c)
    out = pl.pallas_call(
        _mod_kernel,
        grid=(n // tn,),
        in_specs=[pl.BlockSpec((rows, d), lambda j: (0, 0)),
                  pl.BlockSpec((d, tn), lambda j: (0, j)),
                  pl.BlockSpec((1, tn), lambda j: (0, j))],
        out_specs=pl.BlockSpec((rows, tn), lambda j: (0, j)),
        out_shape=jax.ShapeDtypeStruct((rows, n), F32),
        compiler_params=_params(("arbitrary",)),
        name="adaln_mod",
    )(c_pad, w_ada, b_ada.reshape(1, n))
    return out[:b]


def _ln_mod_kernel(x_ref, g_ref, sc_ref, sh_ref, o_ref):
    x = x_ref[...]
    y = x * lax.rsqrt(jnp.mean(x * x, axis=-1, keepdims=True) + EPS) * g_ref[...]
    o_ref[...] = (y * (1.0 + sc_ref[0]) + sh_ref[0]).astype(o_ref.dtype)


def _ln_mod(x2d, g, scale, shift, seq, tm):
    t, d = x2d.shape
    nb = scale.shape[0]
    per = seq // tm
    return pl.pallas_call(
        _ln_mod_kernel,
        grid=(t // tm,),
        in_specs=[pl.BlockSpec((tm, d), lambda i: (i, 0)),
                  pl.BlockSpec((1, d), lambda i: (0, 0)),
                  pl.BlockSpec((1, 1, d), lambda i: (i // per, 0, 0)),
                  pl.BlockSpec((1, 1, d), lambda i: (i // per, 0, 0))],
        out_specs=pl.BlockSpec((tm, d), lambda i: (i, 0)),
        out_shape=jax.ShapeDtypeStruct((t, d), BF16),
        compiler_params=_params(("parallel",)),
        name="ln_mod",
    )(x2d, g.reshape(1, d), scale.reshape(nb, 1, d), shift.reshape(nb, 1, d))


def _pack_kernel(wt_ref, o_ref, *, pieces):
    tr = o_ref.shape[0]
    sub = 4 * LANES

    def emit(rows, dst):
        o_ref[:, dst:dst + rows.shape[0]] = rows.T.astype(o_ref.dtype)

    dst = 0
    pending, pending_rows = [], 0
    for src, width, scale in pieces:
        if width % LANES == 0 and not pending:
            for off in range(0, width, sub):
                n = min(sub, width - off)
                rows = wt_ref[src + off:src + off + n, :]
                emit(rows if scale == 1.0 else rows * scale, dst + off)
            dst += width
            continue
        rows = wt_ref[src:src + width, :]
        pending.append(rows if scale == 1.0 else rows * scale)
        pending_rows += width
        assert pending_rows <= LANES
        if pending_rows == LANES:
            emit(jnp.concatenate(pending, axis=0), dst)
            dst, pending, pending_rows = dst + LANES, [], 0
    if pending:
        pending.append(jnp.zeros((LANES - pending_rows, tr), F32))
        emit(jnp.concatenate(pending, axis=0), dst)
        dst += LANES
    assert dst == o_ref.shape[1]


def _pack_weights(wt, pieces, width, tr):
    ncols, d = wt.shape
    return pl.pallas_call(
        functools.partial(_pack_kernel, pieces=pieces),
        grid=(d // tr,),
        in_specs=[pl.BlockSpec((ncols, tr), lambda i: (0, i))],
        out_specs=pl.BlockSpec((tr, width), lambda i: (i, 0)),
        out_shape=jax.ShapeDtypeStruct((d, width), BF16),
        compiler_params=_params(("parallel",)),
        name="pack_w_in",
    )(wt)


def _inproj_kernel(h_ref, w_ref, o1_ref, o2_ref, *, n1):
    j = pl.program_id(1)
    acc = lambda: jnp.dot(h_ref[...], w_ref[...], preferred_element_type=F32)

    @pl.when(j < n1)
    def _():
        o1_ref[...] = acc().astype(o1_ref.dtype)

    @pl.when(j >= n1)
    def _():
        o2_ref[...] = acc()


def _inproj(h, w, width1, tm, tn):
    t, d = h.shape
    n = w.shape[1]
    n1 = width1 // tn
    return pl.pallas_call(
        functools.partial(_inproj_kernel, n1=n1),
        grid=(t // tm, n // tn),
        in_specs=[pl.BlockSpec((tm, d), lambda i, j: (i, 0)),
                  pl.BlockSpec((d, tn), lambda i, j: (0, j))],
        out_specs=[pl.BlockSpec((tm, tn), lambda i, j: (i, jnp.minimum(j, n1 - 1))),
                   pl.BlockSpec((tm, tn), lambda i, j: (i, jnp.maximum(j - n1, 0)))],
        out_shape=[jax.ShapeDtypeStruct((t, width1), BF16),
                   jax.ShapeDtypeStruct((t, n - width1), F32)],
        compiler_params=_params(("parallel", "arbitrary")),
        name="in_proj",
    )(h, w)


def _mm_relu2_kernel(a_ref, w_ref, o_ref):
    acc = jnp.dot(a_ref[...], w_ref[...], preferred_element_type=F32)
    r = jnp.maximum(acc, 0.0)
    o_ref[...] = (r * r).astype(o_ref.dtype)


def _mm_relu2(a, w, tm, tn):
    t, d = a.shape
    n = w.shape[1]
    return pl.pallas_call(
        _mm_relu2_kernel,
        grid=(t // tm, n // tn),
        in_specs=[pl.BlockSpec((tm, d), lambda i, j: (i, 0)),
                  pl.BlockSpec((d, tn), lambda i, j: (0, j))],
        out_specs=pl.BlockSpec((tm, tn), lambda i, j: (i, j)),
        out_shape=jax.ShapeDtypeStruct((t, n), BF16),
        compiler_params=_params(("parallel", "arbitrary")),
        name="mlp_in",
    )(a, w)


def _mm_resid_kernel(a_ref, w_ref, r_ref, g_ref, fg_ref, o_ref, *, nk):
    k = pl.program_id(1)
    part = lambda: jnp.dot(a_ref[...], w_ref[...], preferred_element_type=F32)

    def finish(acc):
        y = r_ref[...] + g_ref[0] * acc
        o_ref[...] = y * lax.rsqrt(jnp.mean(y * y, axis=-1, keepdims=True) + EPS) * fg_ref[...]

    if nk == 1:
        finish(part())
        return

    @pl.when(k == 0)
    def _():
        o_ref[...] = part()

    @pl.when((k > 0) & (k < nk - 1))
    def _():
        o_ref[...] += part()

    @pl.when(k == nk - 1)
    def _():
        finish(o_ref[...] + part())


def _mm_resid_norm(a, w, resid, gate, norm_g, seq, tm, tk, name):
    t, kdim = a.shape
    n = w.shape[1]
    nb = gate.shape[0]
    per = seq // tm
    nk = kdim // tk
    return pl.pallas_call(
        functools.partial(_mm_resid_kernel, nk=nk),
        grid=(t // tm, nk),
        in_specs=[pl.BlockSpec((tm, tk), lambda i, k: (i, k)),
                  pl.BlockSpec((tk, n), lambda i, k: (k, 0)),
                  pl.BlockSpec((tm, n), lambda i, k: (i, 0)),
                  pl.BlockSpec((1, 1, n), lambda i, k: (i // per, 0, 0)),
                  pl.BlockSpec((1, n), lambda i, k: (0, 0))],
        out_specs=pl.BlockSpec((tm, n), lambda i, k: (i, 0)),
        out_shape=jax.ShapeDtypeStruct((t, n), F32),
        compiler_params=_params(("parallel", "arbitrary"), vmem=VMEM_LIMIT_WIDE),
        name=name,
    )(a, w, resid, gate.reshape(nb, 1, n), norm_g.reshape(1, n))


def _mix_resid_kernel(a1_ref, a2_ref, w1_ref, w2_ref, r_ref, g_ref, o_ref):
    acc = jnp.dot(a1_ref[...], w1_ref[...], preferred_element_type=F32)
    acc += jnp.dot(a2_ref[...], w2_ref[...], preferred_element_type=F32)
    o_ref[...] = r_ref[...] + g_ref[0] * acc


def _mix_resid(a1, a2, w, resid, gate, seq, tm, tn):
    t, k1 = a1.shape
    k2 = a2.shape[1]
    assert k1 == k2
    n = w.shape[1]
    nb = gate.shape[0]
    per = seq // tm
    return pl.pallas_call(
        _mix_resid_kernel,
        grid=(t // tm, n // tn),
        in_specs=[pl.BlockSpec((tm, k1), lambda i, j: (i, 0)),
                  pl.BlockSpec((tm, k2), lambda i, j: (i, 0)),
                  pl.BlockSpec((k1, tn), lambda i, j: (0, j)),
                  pl.BlockSpec((k2, tn), lambda i, j: (1, j)),
                  pl.BlockSpec((tm, tn), lambda i, j: (i, j)),
                  pl.BlockSpec((1, 1, tn), lambda i, j: (i // per, 0, j))],
        out_specs=pl.BlockSpec((tm, tn), lambda i, j: (i, j)),
        out_shape=jax.ShapeDtypeStruct((t, n), F32),
        compiler_params=_params(("parallel", "arbitrary")),
        name="out_proj",
    )(a1, a2, w, w, resid, gate.reshape(nb, 1, n))


def _rope_tables(pos_col, inv_row):
    ang = pos_col.astype(F32) * inv_row
    lane = lax.broadcasted_iota(jnp.int32, ang.shape, 1)
    half = QK_ROPE // 2
    cos = jnp.cos(ang)
    sin = jnp.sin(ang)
    c = jnp.where(lane < QK_ROPE, cos, 0.0)
    s_lo = jnp.where(lane < half, -sin, 0.0)
    s_hi = jnp.where((lane >= half) & (lane < QK_ROPE), sin, 0.0)
    return c, s_lo, s_hi


def _rope_apply(x, c, s_lo, s_hi):
    half = QK_ROPE // 2
    return (x * c + pltpu.roll(x, LANES - half, axis=1) * s_lo
            + pltpu.roll(x, half, axis=1) * s_hi)


def _mla_proj_kernel(cq_ref, ckv_ref, misc_ref, pos_ref, inv_ref, gq_ref, gkv_ref,
                     wqn_ref, wqr_ref, wuk_ref, wuv_ref,
                     qn_ref, qr_ref, kn_ref, vb_ref, kr_ref, *, heads):
    cq = cq_ref[...]
    cqn = (cq * lax.rsqrt(jnp.mean(cq * cq, axis=-1, keepdims=True) + EPS)
           * gq_ref[...]).astype(BF16)
    ckv = ckv_ref[...]
    kvn = (ckv * lax.rsqrt(jnp.mean(ckv * ckv, axis=-1, keepdims=True) + EPS)
           * gkv_ref[...]).astype(BF16)
    c, s_lo, s_hi = _rope_tables(pos_ref[...], inv_ref[...])

    qn_ref[...] = jnp.dot(cqn, wqn_ref[...], preferred_element_type=F32).astype(BF16)
    qr = jnp.dot(cqn, wqr_ref[...], preferred_element_type=F32)
    for h in range(heads):
        sl = slice(h * LANES, (h + 1) * LANES)
        qr_ref[:, sl] = _rope_apply(qr[:, sl], c, s_lo, s_hi).astype(BF16)
    kn_ref[...] = jnp.dot(kvn, wuk_ref[...], preferred_element_type=F32).astype(BF16)
    vb_ref[...] = jnp.dot(kvn, wuv_ref[...], preferred_element_type=F32).astype(BF16)
    kr_ref[...] = _rope_apply(misc_ref[...], c, s_lo, s_hi).astype(BF16)


def _mla_proj(p2, pos_col, inv_row, gq, gkv, wqn, wqr, wuk, wuv, q_lora, kv_lora, heads, tm):
    t = p2.shape[0]
    hw = heads * LANES
    misc_blk = p2.shape[1] // LANES - 1
    const = lambda i: (0, 0)
    return pl.pallas_call(
        functools.partial(_mla_proj_kernel, heads=heads),
        grid=(t // tm,),
        in_specs=[pl.BlockSpec((tm, q_lora), lambda i: (i, 0)),
                  pl.BlockSpec((tm, kv_lora), lambda i: (i, q_lora // kv_lora)),
                  pl.BlockSpec((tm, LANES), lambda i: (i, misc_blk)),
                  pl.BlockSpec((tm, 1), lambda i: (i, 0)),
                  pl.BlockSpec((1, LANES), const),
                  pl.BlockSpec((1, q_lora), const),
                  pl.BlockSpec((1, kv_lora), const),
                  pl.BlockSpec((q_lora, hw), const),
                  pl.BlockSpec((q_lora, hw), const),
                  pl.BlockSpec((kv_lora, hw), const),
                  pl.BlockSpec((kv_lora, hw), const)],
        out_specs=[pl.BlockSpec((tm, hw), lambda i: (i, 0))] * 4
                  + [pl.BlockSpec((tm, LANES), lambda i: (i, 0))],
        out_shape=[jax.ShapeDtypeStruct((t, hw), BF16)] * 4
                  + [jax.ShapeDtypeStruct((t, LANES), BF16)],
        compiler_params=_params(("parallel",)),
        name="mla_proj",
    )(p2, p2, p2, pos_col, inv_row, gq, gkv, wqn, wqr, wuk, wuv)


def _mla_attn_kernel(it_ref, jt_ref, qn_ref, qr_ref, kn_ref, kr_ref, v_ref, qpos_ref, kpos_ref,
                     o_ref, m_ref, l_ref, acc_ref, *, group, tq, tk):
    p = pl.program_id(2)
    i = it_ref[p]
    j = jt_ref[p]

    @pl.when(j == 0)
    def _():
        m_ref[...] = jnp.full(m_ref.shape, MASKED, F32)
        l_ref[...] = jnp.zeros(l_ref.shape, F32)
        acc_ref[...] = jnp.zeros(acc_ref.shape, F32)

    def step(masked):
        kr = kr_ref[...]
        if masked:
            adm = (kpos_ref[...] >> CHUNK_SHIFT) <= (qpos_ref[0] >> CHUNK_SHIFT)
        def scores(g):
            sl = slice(g * LANES, (g + 1) * LANES)
            q2 = jnp.concatenate([qn_ref[:, sl], qr_ref[:, sl]], axis=1)
            k2 = jnp.concatenate([kn_ref[:, sl], kr], axis=1)
            return _dot_nt(k2, q2)

        depth = 4
        ahead = [scores(g) for g in range(min(depth, group))]
        for g in range(group):
            sl = slice(g * LANES, (g + 1) * LANES)
            st = ahead.pop(0)
            if g + depth < group:
                ahead.append(scores(g + depth))
            if masked:
                st = jnp.where(adm, st, MASKED)
            m_old = m_ref[g:g + 1, :]
            m_new = jnp.maximum(m_old, jnp.max(st, axis=0, keepdims=True))
            alpha = jnp.exp2(m_old - m_new)
            pt = jnp.exp2(st - m_new)
            l_ref[g:g + 1, :] = alpha * l_ref[g:g + 1, :] + jnp.sum(pt, axis=0, keepdims=True)
            m_ref[g:g + 1, :] = m_new
            acc_ref[sl, :] = alpha * acc_ref[sl, :] + _dot_tn(v_ref[:, sl], pt.astype(BF16))

    reaches = (j + 1) * tk > i * tq

    @pl.when(reaches)
    def _():
        step(True)

    @pl.when(jnp.logical_not(reaches))
    def _():
        step(False)

    @pl.when(j == ((i + 1) * tq - 1) // tk)
    def _():
        for g in range(group):
            sl = slice(g * LANES, (g + 1) * LANES)
            o_ref[:, sl] = (acc_ref[sl, :] / l_ref[g:g + 1, :]).T.astype(o_ref.dtype)


def _mla_attention(qn, qr, kn, kr, vb, pos_row, pos_col, batch, seq, heads, tq, tk, group):
    nq = seq // tq
    nk = seq // tk
    gw = group * LANES
    pairs = [(i, j) for i in range(nq) for j in range(nk) if j * tk < (i + 1) * tq]
    it = jnp.asarray([p[0] for p in pairs], jnp.int32)
    jt = jnp.asarray([p[1] for p in pairs], jnp.int32)
    qmap = lambda b, h, p, it, jt: (b * nq + it[p], h)
    kmap = lambda b, h, p, it, jt: (b * nk + jt[p], h)
    grid_spec = pltpu.PrefetchScalarGridSpec(
        num_scalar_prefetch=2,
        grid=(batch, heads // group, len(pairs)),
        in_specs=[pl.BlockSpec((tq, gw), qmap),
                  pl.BlockSpec((tq, gw), qmap),
                  pl.BlockSpec((tk, gw), kmap),
                  pl.BlockSpec((tk, LANES), lambda b, h, p, it, jt: (b * nk + jt[p], 0)),
                  pl.BlockSpec((tk, gw), kmap),
                  pl.BlockSpec((1, 1, tq), lambda b, h, p, it, jt: (b, 0, it[p])),
                  pl.BlockSpec((tk, 1), lambda b, h, p, it, jt: (b * nk + jt[p], 0))],
        out_specs=pl.BlockSpec((tq, gw), qmap),
        scratch_shapes=[pltpu.VMEM((group, tq), F32), pltpu.VMEM((group, tq), F32),
                        pltpu.VMEM((gw, tq), F32)],
    )
    return pl.pallas_call(
        functools.partial(_mla_attn_kernel, group=group, tq=tq, tk=tk),
        grid_spec=grid_spec,
        out_shape=jax.ShapeDtypeStruct((batch * seq, heads * LANES), BF16),
        compiler_params=_params(("parallel", "parallel", "arbitrary")),
        name="mla_attn",
    )(it, jt, qn, qr, kn, kr, vb, pos_row, pos_col)


def _tree_sum8(x):
    parts = [x[r:r + 8] for r in range(0, x.shape[0], 8)]
    while len(parts) > 1:
        nxt = [parts[a] + parts[a + 1] for a in range(0, len(parts) - 1, 2)]
        if len(parts) % 2:
            nxt.append(parts[-1])
        parts = nxt
    return parts[0]


def _dsa_kernel(qi_ref, qa_ref, misc_ref, kidx_ref, ka_ref, va_ref, qpos_ref, kpos_ref,
                slopef_ref, o_ref,
                qis_ref, qas_ref, wt_ref, skey_ref, m_ref, l_ref, acc_ref,
                *, tq, kc, idx_heads, heads, topk, hg):
    i = pl.program_id(1)
    nkc = ((i + 1) * tq + kc - 1) // kc

    for h in range(idx_heads):
        qis_ref[h * tq:(h + 1) * tq, :] = qi_ref[:, h * LANES:(h + 1) * LANES]
    for h in range(heads):
        qas_ref[h * tq:(h + 1) * tq, :LANES] = qa_ref[:, h * LANES:(h + 1) * LANES]
    wt_ref[...] = misc_ref[...].T[QK_ROPE:QK_ROPE + idx_heads, :] * (
        (idx_heads ** -0.5) * (HEAD_DIM ** -0.5))
    qpos = qpos_ref[0]
    qchunk = qpos >> CHUNK_SHIFT

    def score_chunk(c, carry):
        r0 = pl.multiple_of(c * kc, kc)
        kx = kidx_ref[pl.ds(r0, kc), :].astype(BF16)
        score = jnp.zeros((kc, tq), F32)
        for g in range(idx_heads // hg):
            lt = _dot_nt(kx, qis_ref[g * hg * tq:(g + 1) * hg * tq, :])
            for hh in range(hg):
                h = g * hg + hh
                score += jnp.maximum(lt[:, hh * tq:(hh + 1) * tq], 0.0) * wt_ref[h:h + 1, :]
        adm = (kpos_ref[pl.ds(r0, kc), :] >> CHUNK_SHIFT) <= qchunk
        bits = pltpu.bitcast(jnp.where(adm, score, -jnp.inf), jnp.int32)
        skey_ref[pl.ds(r0, kc), :] = bits ^ ((bits >> 31) & 0x7FFFFFFF)
        return carry

    lax.fori_loop(0, nkc, score_chunk, 0)

    def count_ge(thr):
        def body(c, cnt):
            r0 = pl.multiple_of(c * kc, kc)
            hit = jnp.where(skey_ref[pl.ds(r0, kc), :] >= thr, 1.0, 0.0)
            return cnt + _tree_sum8(hit)
        cnt8 = lax.fori_loop(0, nkc, body, jnp.zeros((8, tq), F32))
        return jnp.sum(cnt8, axis=0, keepdims=True)

    need = float(topk)
    thr = jnp.where(count_ge(jnp.zeros((1, tq), jnp.int32)) >= need, 0, INT_MIN)

    def bit_step(it, thr):
        cand = thr + lax.shift_left(jnp.int32(1), 30 - it)
        return jnp.where(count_ge(cand) >= need, cand, thr)

    thr = lax.fori_loop(0, 31, bit_step, thr)
    thr = jnp.maximum(thr, NEG_INF_KEY + 1)

    @pl.when(jnp.max(count_ge(thr)) > need)
    def _():
        keep = need - count_ge(thr + 1)

        def ties_below(bound):
            def body(c, cnt):
                r0 = pl.multiple_of(c * kc, kc)
                idx = r0 + lax.broadcasted_iota(jnp.int32, (kc, 1), 0)
                tie = skey_ref[pl.ds(r0, kc), :] == thr
                hit = jnp.where(tie, jnp.where(idx < bound, 1.0, 0.0), 0.0)
                return cnt + _tree_sum8(hit)
            cnt8 = lax.fori_loop(0, nkc, body, jnp.zeros((8, tq), F32))
            return jnp.sum(cnt8, axis=0, keepdims=True)

        nbits = skey_ref.shape[0].bit_length()

        def bound_step(it, below):
            cand = below + lax.shift_left(jnp.int32(1), nbits - 1 - it)
            return jnp.where(ties_below(cand) < keep, cand, below)

        bound = lax.fori_loop(0, nbits, bound_step, jnp.zeros((1, tq), jnp.int32)) + 1

        def demote(c, carry):
            r0 = pl.multiple_of(c * kc, kc)
            idx = r0 + lax.broadcasted_iota(jnp.int32, (kc, 1), 0)
            k = skey_ref[pl.ds(r0, kc), :]
            skey_ref[pl.ds(r0, kc), :] = jnp.where(k == thr, jnp.where(idx >= bound, thr - 1, k), k)
            return carry

        lax.fori_loop(0, nkc, demote, 0)

    m_ref[...] = jnp.full(m_ref.shape, MASKED, F32)
    l_ref[...] = jnp.zeros(l_ref.shape, F32)
    acc_ref[...] = jnp.zeros(acc_ref.shape, F32)

    for h in range(heads):
        qas_ref[h * tq:(h + 1) * tq, LANES:] = jnp.broadcast_to(
            slopef_ref[h:h + 1, :], (tq, LANES)).astype(BF16)
    qpos0 = qpos[:, 0:1]
    lane = lax.broadcasted_iota(jnp.int32, (1, LANES), 1)

    def softmax_update(st, sel, vx, bias):
        m_all = m_ref[...]
        ps, ms, sums = [], [], []
        for h in range(heads):
            sl = slice(h * tq, (h + 1) * tq)
            s = st[:, sl] if bias is None else st[:, sl] - bias(h)
            s = jnp.where(sel, s, MASKED)
            m_new = jnp.maximum(m_all[:, sl], jnp.max(s, axis=0, keepdims=True))
            p = jnp.exp2(s - m_new)
            ms.append(m_new)
            sums.append(jnp.sum(p, axis=0, keepdims=True))
            ps.append(p.astype(BF16))
        m_new = jnp.concatenate(ms, axis=1)
        alpha = jnp.exp2(m_all - m_new)
        m_ref[...] = m_new
        l_ref[...] = alpha * l_ref[...] + jnp.concatenate(sums, axis=1)
        acc_ref[...] = alpha * acc_ref[...] + _dot_tn(vx, jnp.concatenate(ps, axis=1))

    def attend_chunk(c, carry):
        r0 = pl.multiple_of(c * kc, kc)
        rel = kpos_ref[pl.ds(r0, kc), :] - qpos0
        feat = jnp.where(lane < 3, (rel >> CHUNK_SHIFT).astype(F32),
                         jnp.where(lane < 6, (rel & (CHUNK - 1)).astype(F32), 0.0))
        kx = jnp.concatenate([ka_ref[pl.ds(r0, kc), :].astype(BF16), feat.astype(BF16)], axis=1)
        vx = va_ref[pl.ds(r0, kc), :].astype(BF16)
        sel = skey_ref[pl.ds(r0, kc), :] >= jnp.where(rel < 0, thr, jnp.int32(2 ** 31 - 1))
        softmax_update(_dot_nt(kx, qas_ref[...]), sel, vx, None)
        return carry

    lax.fori_loop(0, (i * tq + kc - 1) // kc, attend_chunk, 0)

    r0 = pl.multiple_of(i * tq, tq)
    kp = kpos_ref[pl.ds(r0, tq), :]
    shifted = (jnp.abs(kp - qpos) - (qpos - qpos0)).astype(F32)
    sel = skey_ref[pl.ds(r0, tq), :] >= thr
    st = _dot_nt(ka_ref[pl.ds(r0, tq), :].astype(BF16), qas_ref[:, :LANES])
    softmax_update(st, sel, va_ref[pl.ds(r0, tq), :].astype(BF16),
                   lambda h: (LOG2E * 2.0 ** (-8.0 * (h + 1) / heads)) * shifted)

    for h in range(heads):
        sl = slice(h * tq, (h + 1) * tq)
        o_ref[:, h * LANES:(h + 1) * LANES] = (acc_ref[:, sl] / l_ref[:, sl]).T.astype(o_ref.dtype)


def _alibi_query_features(heads):
    rows = np.zeros((heads, LANES), np.float32)
    for h in range(heads):
        rest = LOG2E * 2.0 ** (-8.0 * (h + 1) / heads)
        for k in range(3):
            piece = float(np.asarray(rest, np.float32).astype(BF16).astype(np.float32))
            rows[h, k] = CHUNK * piece
            rows[h, 3 + k] = piece
            rest -= piece
    return jnp.asarray(rows)


def _dsa(p1, p2, pos_row, pos_col, batch, seq, heads, idx_heads, topk, kv_off, kc):
    tq = LANES
    nq = seq // tq
    hw = heads * LANES
    iw = idx_heads * LANES
    assert iw % hw == 0
    kblk = kv_off // LANES
    misc_blk = p2.shape[1] // LANES - 1
    hg = min(8, idx_heads)
    kernel = functools.partial(_dsa_kernel, tq=tq, kc=kc, idx_heads=idx_heads, heads=heads,
                               topk=topk, hg=hg)
    return pl.pallas_call(
        kernel,
        grid=(batch, nq),
        in_specs=[pl.BlockSpec((tq, iw), lambda b, i: (b * nq + i, 0)),
                  pl.BlockSpec((tq, hw), lambda b, i: (b * nq + i, iw // hw)),
                  pl.BlockSpec((tq, LANES), lambda b, i: (b * nq + i, misc_blk)),
                  pl.BlockSpec((seq, LANES), lambda b, i: (b, kblk + 2)),
                  pl.BlockSpec((seq, LANES), lambda b, i: (b, kblk)),
                  pl.BlockSpec((seq, LANES), lambda b, i: (b, kblk + 1)),
                  pl.BlockSpec((1, 1, tq), lambda b, i: (b, 0, i)),
                  pl.BlockSpec((seq, 1), lambda b, i: (b, 0)),
                  pl.BlockSpec((heads, LANES), lambda b, i: (0, 0))],
        out_specs=pl.BlockSpec((tq, hw), lambda b, i: (b * nq + i, 0)),
        out_shape=jax.ShapeDtypeStruct((batch * seq, hw), BF16),
        scratch_shapes=[pltpu.VMEM((idx_heads * tq, LANES), BF16),
                        pltpu.VMEM((heads * tq, 2 * LANES), BF16),
                        pltpu.VMEM((idx_heads, tq), F32),
                        pltpu.VMEM((seq, tq), jnp.int32),
                        pltpu.VMEM((1, heads * tq), F32),
                        pltpu.VMEM((1, heads * tq), F32),
                        pltpu.VMEM((LANES, heads * tq), F32)],
        compiler_params=_params(("parallel", "arbitrary")),
        name="dsa",
    )(p1, p1, p2, p2, p2, p2, pos_row, pos_col, _alibi_query_features(heads))


def _tile(n, want):
    t = min(n, want)
    assert n % t == 0, (n, want)
    return t


def _block(x, c, positions, w_ada, b_ada, ln1_g, w_in, q_norm_g, kv_norm_g, w_uq, w_uk, w_uv,
           w_o, ln2_g, w_mlp_in, w_mlp_out, final_g, *, heads, idx_heads, b_heads):
    batch, seq, d = x.shape
    t = batch * seq
    q_lora = w_uq.shape[0]
    kv_lora = w_uk.shape[0]
    d_ff = w_mlp_in.shape[1]
    hw = heads * HEAD_DIM
    iw = idx_heads * HEAD_DIM
    bw = b_heads * HEAD_DIM
    topk = min(TOPK_MAX, seq // 4)
    assert (1 << CHUNK_SHIFT) == CHUNK and QK_ROPE + idx_heads <= LANES

    splits = (hw, HEAD_DIM, HEAD_DIM, iw, HEAD_DIM, idx_heads, q_lora, kv_lora, QK_ROPE)
    offs = [0]
    for s in splits:
        offs.append(offs[-1] + s)
    piece = lambda k, scale=1.0: (offs[k], splits[k], scale)
    pieces = (piece(3), piece(0, HEAD_DIM ** -0.5 * LOG2E), piece(6), piece(7), piece(1),
              piece(2), piece(4), piece(8), piece(5))
    packed_width = -(-offs[-1] // LANES) * LANES
    w_pack = _pack_weights(w_in.T, pieces, packed_width, _tile(d, 256))
    width1 = hw + iw
    kv_off = q_lora + kv_lora

    wq = w_uq.reshape(q_lora, b_heads, HEAD_DIM + QK_ROPE) * ((HEAD_DIM + QK_ROPE) ** -0.5 * LOG2E)
    wqn = wq[:, :, :HEAD_DIM].reshape(q_lora, bw).astype(BF16)
    wqr = jnp.pad(wq[:, :, HEAD_DIM:], ((0, 0), (0, 0), (0, LANES - QK_ROPE))
                  ).reshape(q_lora, b_heads * LANES).astype(BF16)
    wuk = w_uk.astype(BF16)
    wuv = w_uv.astype(BF16)
    wo = w_o.astype(BF16)
    w1 = w_mlp_in.astype(BF16)
    w2 = w_mlp_out.astype(BF16)

    lane = jnp.arange(LANES)
    inv_row = (ROPE_THETA ** (-(2.0 * (lane % (QK_ROPE // 2))).astype(F32) / QK_ROPE)
               ).reshape(1, LANES)
    pos_row = positions.reshape(batch, 1, seq)
    pos_col = positions.reshape(t, 1)
    x2d = x.reshape(t, d)

    mod = _adaln_mod(c, w_ada, b_ada, _tile(6 * d, 512))
    shift1, scale1, gate1, shift2, scale2, gate2 = jnp.split(mod, 6, axis=-1)

    tm_ln = _tile(seq, 512)
    h = _ln_mod(x2d, ln1_g, scale1, shift1, seq, tm_ln)
    p1, p2 = _inproj(h, w_pack, width1, _tile(t, 1024), _tile(math.gcd(width1, w_pack.shape[1] - width1), 1024))

    out_a = _dsa(p1, p2, pos_row, pos_col, batch, seq, heads, idx_heads, topk, kv_off,
                 _tile(seq, 512))

    qn, qr, kn, vb, kr = _mla_proj(p2, pos_col, inv_row, q_norm_g.reshape(1, q_lora),
                                   kv_norm_g.reshape(1, kv_lora), wqn, wqr, wuk, wuv,
                                   q_lora, kv_lora, b_heads, _tile(t, 512))
    out_b = _mla_attention(qn, qr, kn, kr, vb, pos_row, pos_col, batch, seq, b_heads,
                           _tile(seq, 512), _tile(seq, 512), min(b_heads, 16))

    tm = _tile(seq, 1024)
    x1 = _mix_resid(out_a, out_b, wo, x2d, gate1, seq, tm, _tile(d, 1024))
    h2 = _ln_mod(x1, ln2_g, scale2, shift2, seq, tm_ln)
    hid = _mm_relu2(h2, w1, tm, _tile(d_ff, 1024))
    out = _mm_resid_norm(hid, w2, x1, gate2, final_g, seq, _tile(seq, 512), _tile(d_ff, 1024),
                         "mlp_out")
    return out.reshape(batch, seq, d)


def kernel(x, c, positions, w_ada, b_ada, ln1_g, w_in, q_norm_g, kv_norm_g, w_uq, w_uk, w_uv,
           w_o, ln2_g, w_mlp_in, w_mlp_out, final_g):
    assert w_ada.shape[0] == 1, "single-layer block"
    return _block(x, c, positions, w_ada[0], b_ada[0], ln1_g[0], w_in[0], q_norm_g[0],
                  kv_norm_g[0], w_uq[0], w_uk[0], w_uv[0], w_o[0], ln2_g[0], w_mlp_in[0],
                  w_mlp_out[0], final_g, heads=A_HEADS, idx_heads=IDX_HEADS, b_heads=B_HEADS)
```

```python
import functools
import math

import jax
import jax.numpy as jnp
import numpy as np
from jax import lax
from jax.experimental import pallas as pl
from jax.experimental.pallas import tpu as pltpu

F32 = jnp.float32
BF16 = jnp.bfloat16

EPS = 1e-6
CHUNK = 64
CHUNK_SHIFT = 6
A_HEADS = 16
IDX_HEADS = 32
HEAD_DIM = 128
TOPK_MAX = 256
B_HEADS = 16
QK_ROPE = 64
ROPE_THETA = 10000.0

LANES = 128
LOG2E = math.log2(math.e)
MASKED = -1e30
INT_MIN = -(2 ** 31)
NEG_INF_KEY = -2139095041
VMEM_LIMIT = 56 * 1024 * 1024
VMEM_LIMIT_WIDE = 60 * 1024 * 1024


def _params(semantics, vmem=VMEM_LIMIT):
    return pltpu.CompilerParams(dimension_semantics=semantics, vmem_limit_bytes=vmem)


def _dot_nt(a, b):
    return lax.dot_general(a, b, (((1,), (1,)), ((), ())), preferred_element_type=F32)


def _dot_tn(a, b):
    return lax.dot_general(a, b, (((0,), (0,)), ((), ())), preferred_element_type=F32)


def _mod_kernel(c_ref, w_ref, b_ref, o_ref):
    c = c_ref[...]
    s = c * (1.0 / (1.0 + jnp.exp(-c)))
    o_ref[...] = jnp.dot(s.astype(BF16), w_ref[...].astype(BF16),
                         preferred_element_type=F32) + b_ref[...]


def _adaln_mod(c, w_ada, b_ada, tn):
    b, d = c.shape
    n = w_ada.shape[1]
    rows = 8
    c_pad = jnp.zeros((rows, d), F32).at[:b].set(c)
    out = pl.pallas_call(
        _mod_kernel,
        grid=(n // tn,),
        in_specs=[pl.BlockSpec((rows, d), lambda j: (0, 0)),
                  pl.BlockSpec((d, tn), lambda j: (0, j)),
                  pl.BlockSpec((1, tn), lambda j: (0, j))],
        out_specs=pl.BlockSpec((rows, tn), lambda j: (0, j)),
        out_shape=jax.ShapeDtypeStruct((rows, n), F32),
        compiler_params=_params(("arbitrary",)),
        name="adaln_mod",
    )(c_pad, w_ada, b_ada.reshape(1, n))
    return out[:b]


def _ln_mod_kernel(x_ref, g_ref, sc_ref, sh_ref, o_ref):
    x = x_ref[...]
    y = x * lax.rsqrt(jnp.mean(x * x, axis=-1, keepdims=True) + EPS) * g_ref[...]
    o_ref[...] = (y * (1.0 + sc_ref[0]) + sh_ref[0]).astype(o_ref.dtype)


def _ln_mod(x2d, g, scale, shift, seq, tm):
    t, d = x2d.shape
    nb = scale.shape[0]
    per = seq // tm
    return pl.pallas_call(
        _ln_mod_kernel,
        grid=(t // tm,),
        in_specs=[pl.BlockSpec((tm, d), lambda i: (i, 0)),
                  pl.BlockSpec((1, d), lambda i: (0, 0)),
                  pl.BlockSpec((1, 1, d), lambda i: (i // per, 0, 0)),
                  pl.BlockSpec((1, 1, d), lambda i: (i // per, 0, 0))],
        out_specs=pl.BlockSpec((tm, d), lambda i: (i, 0)),
        out_shape=jax.ShapeDtypeStruct((t, d), BF16),
        compiler_params=_params(("parallel",)),
        name="ln_mod",
    )(x2d, g.reshape(1, d), scale.reshape(nb, 1, d), shift.reshape(nb, 1, d))


def _pack_kernel(wt_ref, o_ref, *, pieces):
    tr = o_ref.shape[0]
    sub = 4 * LANES

    def emit(rows, dst):
        o_ref[:, dst:dst + rows.shape[0]] = rows.T.astype(o_ref.dtype)

    dst = 0
    pending, pending_rows = [], 0
    for src, width, scale in pieces:
        if width % LANES == 0 and not pending:
            for off in range(0, width, sub):
                n = min(sub, width - off)
                rows = wt_ref[src + off:src + off + n, :]
                emit(rows if scale == 1.0 else rows * scale, dst + off)
            dst += width
            continue
        rows = wt_ref[src:src + width, :]
        pending.append(rows if scale == 1.0 else rows * scale)
        pending_rows += width
        assert pending_rows <= LANES
        if pending_rows == LANES:
            emit(jnp.concatenate(pending, axis=0), dst)
            dst, pending, pending_rows = dst + LANES, [], 0
    if pending:
        pending.append(jnp.zeros((LANES - pending_rows, tr), F32))
        emit(jnp.concatenate(pending, axis=0), dst)
        dst += LANES
    assert dst == o_ref.shape[1]


def _pack_weights(wt, pieces, width, tr):
    ncols, d = wt.shape
    return pl.pallas_call(
        functools.partial(_pack_kernel, pieces=pieces),
        grid=(d // tr,),
        in_specs=[pl.BlockSpec((ncols, tr), lambda i: (0, i))],
        out_specs=pl.BlockSpec((tr, width), lambda i: (i, 0)),
        out_shape=jax.ShapeDtypeStruct((d, width), BF16),
        compiler_params=_params(("parallel",)),
        name="pack_w_in",
    )(wt)


def _inproj_kernel(h_ref, w_ref, o1_ref, o2_ref, *, n1):
    j = pl.program_id(1)
    acc = lambda: jnp.dot(h_ref[...], w_ref[...], preferred_element_type=F32)

    @pl.when(j < n1)
    def _():
        o1_ref[...] = acc().astype(o1_ref.dtype)

    @pl.when(j >= n1)
    def _():
        o2_ref[...] = acc()


def _inproj(h, w, width1, tm, tn):
    t, d = h.shape
    n = w.shape[1]
    n1 = width1 // tn
    return pl.pallas_call(
        functools.partial(_inproj_kernel, n1=n1),
        grid=(t // tm, n // tn),
        in_specs=[pl.BlockSpec((tm, d), lambda i, j: (i, 0)),
                  pl.BlockSpec((d, tn), lambda i, j: (0, j))],
        out_specs=[pl.BlockSpec((tm, tn), lambda i, j: (i, jnp.minimum(j, n1 - 1))),
                   pl.BlockSpec((tm, tn), lambda i, j: (i, jnp.maximum(j - n1, 0)))],
        out_shape=[jax.ShapeDtypeStruct((t, width1), BF16),
                   jax.ShapeDtypeStruct((t, n - width1), F32)],
        compiler_params=_params(("parallel", "arbitrary")),
        name="in_proj",
    )(h, w)


def _mm_relu2_kernel(a_ref, w_ref, o_ref):
    acc = jnp.dot(a_ref[...], w_ref[...], preferred_element_type=F32)
    r = jnp.maximum(acc, 0.0)
    o_ref[...] = (r * r).astype(o_ref.dtype)


def _mm_relu2(a, w, tm, tn):
    t, d = a.shape
    n = w.shape[1]
    return pl.pallas_call(
        _mm_relu2_kernel,
        grid=(t // tm, n // tn),
        in_specs=[pl.BlockSpec((tm, d), lambda i, j: (i, 0)),
                  pl.BlockSpec((d, tn), lambda i, j: (0, j))],
        out_specs=pl.BlockSpec((tm, tn), lambda i, j: (i, j)),
        out_shape=jax.ShapeDtypeStruct((t, n), BF16),
        compiler_params=_params(("parallel", "arbitrary")),
        name="mlp_in",
    )(a, w)


def _mm_resid_kernel(a_ref, w_ref, r_ref, g_ref, fg_ref, o_ref, *, nk):
    k = pl.program_id(1)
    part = lambda: jnp.dot(a_ref[...], w_ref[...], preferred_element_type=F32)

    def finish(acc):
        y = r_ref[...] + g_ref[0] * acc
        o_ref[...] = y * lax.rsqrt(jnp.mean(y * y, axis=-1, keepdims=True) + EPS) * fg_ref[...]

    if nk == 1:
        finish(part())
        return

    @pl.when(k == 0)
    def _():
        o_ref[...] = part()

    @pl.when((k > 0) & (k < nk - 1))
    def _():
        o_ref[...] += part()

    @pl.when(k == nk - 1)
    def _():
        finish(o_ref[...] + part())


def _mm_resid_norm(a, w, resid, gate, norm_g, seq, tm, tk, name):
    t, kdim = a.shape
    n = w.shape[1]
    nb = gate.shape[0]
    per = seq // tm
    nk = kdim // tk
    return pl.pallas_call(
        functools.partial(_mm_resid_kernel, nk=nk),
        grid=(t // tm, nk),
        in_specs=[pl.BlockSpec((tm, tk), lambda i, k: (i, k)),
                  pl.BlockSpec((tk, n), lambda i, k: (k, 0)),
                  pl.BlockSpec((tm, n), lambda i, k: (i, 0)),
                  pl.BlockSpec((1, 1, n), lambda i, k: (i // per, 0, 0)),
                  pl.BlockSpec((1, n), lambda i, k: (0, 0))],
        out_specs=pl.BlockSpec((tm, n), lambda i, k: (i, 0)),
        out_shape=jax.ShapeDtypeStruct((t, n), F32),
        compiler_params=_params(("parallel", "arbitrary"), vmem=VMEM_LIMIT_WIDE),
        name=name,
    )(a, w, resid, gate.reshape(nb, 1, n), norm_g.reshape(1, n))


def _mix_resid_kernel(a1_ref, a2_ref, w1_ref, w2_ref, r_ref, g_ref, o_ref):
    acc = jnp.dot(a1_ref[...], w1_ref[...], preferred_element_type=F32)
    acc += jnp.dot(a2_ref[...], w2_ref[...], preferred_element_type=F32)
    o_ref[...] = r_ref[...] + g_ref[0] * acc


def _mix_resid(a1, a2, w, resid, gate, seq, tm, tn):
    t, k1 = a1.shape
    k2 = a2.shape[1]
    assert k1 == k2
    n = w.shape[1]
    nb = gate.shape[0]
    per = seq // tm
    return pl.pallas_call(
        _mix_resid_kernel,
        grid=(t // tm, n // tn),
        in_specs=[pl.BlockSpec((tm, k1), lambda i, j: (i, 0)),
                  pl.BlockSpec((tm, k2), lambda i, j: (i, 0)),
                  pl.BlockSpec((k1, tn), lambda i, j: (0, j)),
                  pl.BlockSpec((k2, tn), lambda i, j: (1, j)),
                  pl.BlockSpec((tm, tn), lambda i, j: (i, j)),
                  pl.BlockSpec((1, 1, tn), lambda i, j: (i // per, 0, j))],
        out_specs=pl.BlockSpec((tm, tn), lambda i, j: (i, j)),
        out_shape=jax.ShapeDtypeStruct((t, n), F32),
        compiler_params=_params(("parallel", "arbitrary")),
        name="out_proj",
    )(a1, a2, w, w, resid, gate.reshape(nb, 1, n))


def _rope_tables(pos_col, inv_row):
    ang = pos_col.astype(F32) * inv_row
    lane = lax.broadcasted_iota(jnp.int32, ang.shape, 1)
    half = QK_ROPE // 2
    cos = jnp.cos(ang)
    sin = jnp.sin(ang)
    c = jnp.where(lane < QK_ROPE, cos, 0.0)
    s_lo = jnp.where(lane < half, -sin, 0.0)
    s_hi = jnp.where((lane >= half) & (lane < QK_ROPE), sin, 0.0)
    return c, s_lo, s_hi


def _rope_apply(x, c, s_lo, s_hi):
    half = QK_ROPE // 2
    return (x * c + pltpu.roll(x, LANES - half, axis=1) * s_lo
            + pltpu.roll(x, half, axis=1) * s_hi)


def _mla_proj_kernel(cq_ref, ckv_ref, misc_ref, pos_ref, inv_ref, gq_ref, gkv_ref,
                     wqn_ref, wqr_ref, wuk_ref, wuv_ref,
                     qn_ref, qr_ref, kn_ref, vb_ref, kr_ref, *, heads):
    cq = cq_ref[...]
    cqn = (cq * lax.rsqrt(jnp.mean(cq * cq, axis=-1, keepdims=True) + EPS)
           * gq_ref[...]).astype(BF16)
    ckv = ckv_ref[...]
    kvn = (ckv * lax.rsqrt(jnp.mean(ckv * ckv, axis=-1, keepdims=True) + EPS)
           * gkv_ref[...]).astype(BF16)
    c, s_lo, s_hi = _rope_tables(pos_ref[...], inv_ref[...])

    qn_ref[...] = jnp.dot(cqn, wqn_ref[...], preferred_element_type=F32).astype(BF16)
    qr = jnp.dot(cqn, wqr_ref[...], preferred_element_type=F32)
    for h in range(heads):
        sl = slice(h * LANES, (h + 1) * LANES)
        qr_ref[:, sl] = _rope_apply(qr[:, sl], c, s_lo, s_hi).astype(BF16)
    kn_ref[...] = jnp.dot(kvn, wuk_ref[...], preferred_element_type=F32).astype(BF16)
    vb_ref[...] = jnp.dot(kvn, wuv_ref[...], preferred_element_type=F32).astype(BF16)
    kr_ref[...] = _rope_apply(misc_ref[...], c, s_lo, s_hi).astype(BF16)


def _mla_proj(p2, pos_col, inv_row, gq, gkv, wqn, wqr, wuk, wuv, q_lora, kv_lora, heads, tm):
    t = p2.shape[0]
    hw = heads * LANES
    misc_blk = p2.shape[1] // LANES - 1
    const = lambda i: (0, 0)
    return pl.pallas_call(
        functools.partial(_mla_proj_kernel, heads=heads),
        grid=(t // tm,),
        in_specs=[pl.BlockSpec((tm, q_lora), lambda i: (i, 0)),
                  pl.BlockSpec((tm, kv_lora), lambda i: (i, q_lora // kv_lora)),
                  pl.BlockSpec((tm, LANES), lambda i: (i, misc_blk)),
                  pl.BlockSpec((tm, 1), lambda i: (i, 0)),
                  pl.BlockSpec((1, LANES), const),
                  pl.BlockSpec((1, q_lora), const),
                  pl.BlockSpec((1, kv_lora), const),
                  pl.BlockSpec((q_lora, hw), const),
                  pl.BlockSpec((q_lora, hw), const),
                  pl.BlockSpec((kv_lora, hw), const),
                  pl.BlockSpec((kv_lora, hw), const)],
        out_specs=[pl.BlockSpec((tm, hw), lambda i: (i, 0))] * 4
                  + [pl.BlockSpec((tm, LANES), lambda i: (i, 0))],
        out_shape=[jax.ShapeDtypeStruct((t, hw), BF16)] * 4
                  + [jax.ShapeDtypeStruct((t, LANES), BF16)],
        compiler_params=_params(("parallel",)),
        name="mla_proj",
    )(p2, p2, p2, pos_col, inv_row, gq, gkv, wqn, wqr, wuk, wuv)


def _mla_attn_kernel(it_ref, jt_ref, qn_ref, qr_ref, kn_ref, kr_ref, v_ref, qpos_ref, kpos_ref,
                     o_ref, m_ref, l_ref, acc_ref, *, group, tq, tk):
    p = pl.program_id(2)
    i = it_ref[p]
    j = jt_ref[p]

    @pl.when(j == 0)
    def _():
        m_ref[...] = jnp.full(m_ref.shape, MASKED, F32)
        l_ref[...] = jnp.zeros(l_ref.shape, F32)
        acc_ref[...] = jnp.zeros(acc_ref.shape, F32)

    def step(masked):
        kr = kr_ref[...]
        if masked:
            adm = (kpos_ref[...] >> CHUNK_SHIFT) <= (qpos_ref[0] >> CHUNK_SHIFT)
        def scores(g):
            sl = slice(g * LANES, (g + 1) * LANES)
            q2 = jnp.concatenate([qn_ref[:, sl], qr_ref[:, sl]], axis=1)
            k2 = jnp.concatenate([kn_ref[:, sl], kr], axis=1)
            return _dot_nt(k2, q2)

        depth = 4
        ahead = [scores(g) for g in range(min(depth, group))]
        for g in range(group):
            sl = slice(g * LANES, (g + 1) * LANES)
            st = ahead.pop(0)
            if g + depth < group:
                ahead.append(scores(g + depth))
            if masked:
                st = jnp.where(adm, st, MASKED)
            m_old = m_ref[g:g + 1, :]
            m_new = jnp.maximum(m_old, jnp.max(st, axis=0, keepdims=True))
            alpha = jnp.exp2(m_old - m_new)
            pt = jnp.exp2(st - m_new)
            l_ref[g:g + 1, :] = alpha * l_ref[g:g + 1, :] + jnp.sum(pt, axis=0, keepdims=True)
            m_ref[g:g + 1, :] = m_new
            acc_ref[sl, :] = alpha * acc_ref[sl, :] + _dot_tn(v_ref[:, sl], pt.astype(BF16))

    reaches = (j + 1) * tk > i * tq

    @pl.when(reaches)
    def _():
        step(True)

    @pl.when(jnp.logical_not(reaches))
    def _():
        step(False)

    @pl.when(j == ((i + 1) * tq - 1) // tk)
    def _():
        for g in range(group):
            sl = slice(g * LANES, (g + 1) * LANES)
            o_ref[:, sl] = (acc_ref[sl, :] / l_ref[g:g + 1, :]).T.astype(o_ref.dtype)


def _mla_attention(qn, qr, kn, kr, vb, pos_row, pos_col, batch, seq, heads, tq, tk, group):
    nq = seq // tq
    nk = seq // tk
    gw = group * LANES
    pairs = [(i, j) for i in range(nq) for j in range(nk) if j * tk < (i + 1) * tq]
    it = jnp.asarray([p[0] for p in pairs], jnp.int32)
    jt = jnp.asarray([p[1] for p in pairs], jnp.int32)
    qmap = lambda b, h, p, it, jt: (b * nq + it[p], h)
    kmap = lambda b, h, p, it, jt: (b * nk + jt[p], h)
    grid_spec = pltpu.PrefetchScalarGridSpec(
        num_scalar_prefetch=2,
        grid=(batch, heads // group, len(pairs)),
        in_specs=[pl.BlockSpec((tq, gw), qmap),
                  pl.BlockSpec((tq, gw), qmap),
                  pl.BlockSpec((tk, gw), kmap),
                  pl.BlockSpec((tk, LANES), lambda b, h, p, it, jt: (b * nk + jt[p], 0)),
                  pl.BlockSpec((tk, gw), kmap),
                  pl.BlockSpec((1, 1, tq), lambda b, h, p, it, jt: (b, 0, it[p])),
                  pl.BlockSpec((tk, 1), lambda b, h, p, it, jt: (b * nk + jt[p], 0))],
        out_specs=pl.BlockSpec((tq, gw), qmap),
        scratch_shapes=[pltpu.VMEM((group, tq), F32), pltpu.VMEM((group, tq), F32),
                        pltpu.VMEM((gw, tq), F32)],
    )
    return pl.pallas_call(
        functools.partial(_mla_attn_kernel, group=group, tq=tq, tk=tk),
        grid_spec=grid_spec,
        out_shape=jax.ShapeDtypeStruct((batch * seq, heads * LANES), BF16),
        compiler_params=_params(("parallel", "parallel", "arbitrary")),
        name="mla_attn",
    )(it, jt, qn, qr, kn, kr, vb, pos_row, pos_col)


def _tree_sum8(x):
    parts = [x[r:r + 8] for r in range(0, x.shape[0], 8)]
    while len(parts) > 1:
        nxt = [parts[a] + parts[a + 1] for a in range(0, len(parts) - 1, 2)]
        if len(parts) % 2:
            nxt.append(parts[-1])
        parts = nxt
    return parts[0]


def _dsa_kernel(qi_ref, qa_ref, misc_ref, kidx_ref, ka_ref, va_ref, qpos_ref, kpos_ref,
                slopef_ref, o_ref,
                qis_ref, qas_ref, wt_ref, sc_ref, m_ref, l_ref, acc_ref,
                *, tq, kc, idx_heads, heads, topk, hg):
    i = pl.program_id(1)
    nkc = ((i + 1) * tq + kc - 1) // kc

    for h in range(idx_heads):
        qis_ref[h * tq:(h + 1) * tq, :] = qi_ref[:, h * LANES:(h + 1) * LANES]
    for h in range(heads):
        qas_ref[h * tq:(h + 1) * tq, :LANES] = qa_ref[:, h * LANES:(h + 1) * LANES]
    wt_ref[...] = misc_ref[...].T[QK_ROPE:QK_ROPE + idx_heads, :] * (
        (idx_heads ** -0.5) * (HEAD_DIM ** -0.5))
    qpos = qpos_ref[0]
    qchunk = qpos >> CHUNK_SHIFT

    def score_chunk(c, carry):
        r0 = pl.multiple_of(c * kc, kc)
        kx = kidx_ref[pl.ds(r0, kc), :].astype(BF16)
        score = jnp.zeros((kc, tq), F32)
        for g in range(idx_heads // hg):
            lt = _dot_nt(kx, qis_ref[g * hg * tq:(g + 1) * hg * tq, :])
            for hh in range(hg):
                h = g * hg + hh
                score += jnp.maximum(lt[:, hh * tq:(hh + 1) * tq], 0.0) * wt_ref[h:h + 1, :]
        adm = (kpos_ref[pl.ds(r0, kc), :] >> CHUNK_SHIFT) <= qchunk
        sc_ref[pl.ds(r0, kc), :] = jnp.where(adm, score, -jnp.inf)
        return carry

    lax.fori_loop(0, nkc, score_chunk, 0)

    def key_to_float(key):
        return pltpu.bitcast(key ^ ((key >> 31) & 0x7FFFFFFF), F32)

    def count_ge(thr):
        def body(c, cnt):
            r0 = pl.multiple_of(c * kc, kc)
            hit = jnp.where(sc_ref[pl.ds(r0, kc), :] >= thr, 1.0, 0.0)
            return cnt + _tree_sum8(hit)
        cnt8 = lax.fori_loop(0, nkc, body, jnp.zeros((8, tq), F32))
        return jnp.sum(cnt8, axis=0, keepdims=True)

    need = float(topk)
    key = jnp.where(count_ge(jnp.zeros((1, tq), F32)) >= need, 0, INT_MIN)

    def bit_step(it, key):
        cand = key + lax.shift_left(jnp.int32(1), 30 - it)
        return jnp.where(count_ge(key_to_float(cand)) >= need, cand, key)

    key = lax.fori_loop(0, 31, bit_step, key)
    key = jnp.maximum(key, NEG_INF_KEY + 1)
    thr = key_to_float(key)

    @pl.when(jnp.max(count_ge(thr)) > need)
    def _():
        keep = need - count_ge(key_to_float(key + 1))

        def ties_below(bound):
            def body(c, cnt):
                r0 = pl.multiple_of(c * kc, kc)
                idx = r0 + lax.broadcasted_iota(jnp.int32, (kc, 1), 0)
                tie = sc_ref[pl.ds(r0, kc), :] == thr
                hit = jnp.where(tie, jnp.where(idx < bound, 1.0, 0.0), 0.0)
                return cnt + _tree_sum8(hit)
            cnt8 = lax.fori_loop(0, nkc, body, jnp.zeros((8, tq), F32))
            return jnp.sum(cnt8, axis=0, keepdims=True)

        nbits = sc_ref.shape[0].bit_length()

        def bound_step(it, below):
            cand = below + lax.shift_left(jnp.int32(1), nbits - 1 - it)
            return jnp.where(ties_below(cand) < keep, cand, below)

        bound = lax.fori_loop(0, nbits, bound_step, jnp.zeros((1, tq), jnp.int32)) + 1

        def demote(c, carry):
            r0 = pl.multiple_of(c * kc, kc)
            idx = r0 + lax.broadcasted_iota(jnp.int32, (kc, 1), 0)
            sc = sc_ref[pl.ds(r0, kc), :]
            sc_ref[pl.ds(r0, kc), :] = jnp.where(
                sc == thr, jnp.where(idx >= bound, -jnp.inf, sc), sc)
            return carry

        lax.fori_loop(0, nkc, demote, 0)

    m_ref[...] = jnp.full(m_ref.shape, MASKED, F32)
    l_ref[...] = jnp.zeros(l_ref.shape, F32)
    acc_ref[...] = jnp.zeros(acc_ref.shape, F32)

    for h in range(heads):
        qas_ref[h * tq:(h + 1) * tq, LANES:] = jnp.broadcast_to(
            slopef_ref[h:h + 1, :], (tq, LANES)).astype(BF16)
    qpos0 = qpos[:, 0:1]
    lane = lax.broadcasted_iota(jnp.int32, (1, LANES), 1)

    def softmax_update(st, sel, vx, bias):
        m_all = m_ref[...]
        ps, ms, sums = [], [], []
        for h in range(heads):
            sl = slice(h * tq, (h + 1) * tq)
            s = st[:, sl] if bias is None else st[:, sl] - bias(h)
            s = jnp.where(sel, s, MASKED)
            m_new = jnp.maximum(m_all[:, sl], jnp.max(s, axis=0, keepdims=True))
            p = jnp.exp2(s - m_new)
            ms.append(m_new)
            sums.append(jnp.sum(p, axis=0, keepdims=True))
            ps.append(p.astype(BF16))
        m_new = jnp.concatenate(ms, axis=1)
        alpha = jnp.exp2(m_all - m_new)
        m_ref[...] = m_new
        l_ref[...] = alpha * l_ref[...] + jnp.concatenate(sums, axis=1)
        acc_ref[...] = alpha * acc_ref[...] + _dot_tn(vx, jnp.concatenate(ps, axis=1))

    def attend_chunk(c, carry):
        r0 = pl.multiple_of(c * kc, kc)
        rel = kpos_ref[pl.ds(r0, kc), :] - qpos0
        feat = jnp.where(lane < 3, (rel >> CHUNK_SHIFT).astype(F32),
                         jnp.where(lane < 6, (rel & (CHUNK - 1)).astype(F32), 0.0))
        kx = jnp.concatenate([ka_ref[pl.ds(r0, kc), :].astype(BF16), feat.astype(BF16)], axis=1)
        vx = va_ref[pl.ds(r0, kc), :].astype(BF16)
        sel = sc_ref[pl.ds(r0, kc), :] >= jnp.where(rel < 0, thr, jnp.inf)
        softmax_update(_dot_nt(kx, qas_ref[...]), sel, vx, None)
        return carry

    lax.fori_loop(0, (i * tq + kc - 1) // kc, attend_chunk, 0)

    r0 = pl.multiple_of(i * tq, tq)
    kp = kpos_ref[pl.ds(r0, tq), :]
    shifted = (jnp.abs(kp - qpos) - (qpos - qpos0)).astype(F32)
    sel = sc_ref[pl.ds(r0, tq), :] >= thr
    st = _dot_nt(ka_ref[pl.ds(r0, tq), :].astype(BF16), qas_ref[:, :LANES])
    softmax_update(st, sel, va_ref[pl.ds(r0, tq), :].astype(BF16),
                   lambda h: (LOG2E * 2.0 ** (-8.0 * (h + 1) / heads)) * shifted)

    for h in range(heads):
        sl = slice(h * tq, (h + 1) * tq)
        o_ref[:, h * LANES:(h + 1) * LANES] = (acc_ref[:, sl] / l_ref[:, sl]).T.astype(o_ref.dtype)


def _alibi_query_features(heads):
    rows = np.zeros((heads, LANES), np.float32)
    for h in range(heads):
        rest = LOG2E * 2.0 ** (-8.0 * (h + 1) / heads)
        for k in range(3):
            piece = float(np.asarray(rest, np.float32).astype(BF16).astype(np.float32))
            rows[h, k] = CHUNK * piece
            rows[h, 3 + k] = piece
            rest -= piece
    return jnp.asarray(rows)


def _dsa(p1, p2, pos_row, pos_col, batch, seq, heads, idx_heads, topk, kv_off, kc):
    tq = LANES
    nq = seq // tq
    hw = heads * LANES
    iw = idx_heads * LANES
    assert iw % hw == 0
    kblk = kv_off // LANES
    misc_blk = p2.shape[1] // LANES - 1
    hg = min(8, idx_heads)
    kernel = functools.partial(_dsa_kernel, tq=tq, kc=kc, idx_heads=idx_heads, heads=heads,
                               topk=topk, hg=hg)
    return pl.pallas_call(
        kernel,
        grid=(batch, nq),
        in_specs=[pl.BlockSpec((tq, iw), lambda b, i: (b * nq + i, 0)),
                  pl.BlockSpec((tq, hw), lambda b, i: (b * nq + i, iw // hw)),
                  pl.BlockSpec((tq, LANES), lambda b, i: (b * nq + i, misc_blk)),
                  pl.BlockSpec((seq, LANES), lambda b, i: (b, kblk + 2)),
                  pl.BlockSpec((seq, LANES), lambda b, i: (b, kblk)),
                  pl.BlockSpec((seq, LANES), lambda b, i: (b, kblk + 1)),
                  pl.BlockSpec((1, 1, tq), lambda b, i: (b, 0, i)),
                  pl.BlockSpec((seq, 1), lambda b, i: (b, 0)),
                  pl.BlockSpec((heads, LANES), lambda b, i: (0, 0))],
        out_specs=pl.BlockSpec((tq, hw), lambda b, i: (b * nq + i, 0)),
        out_shape=jax.ShapeDtypeStruct((batch * seq, hw), BF16),
        scratch_shapes=[pltpu.VMEM((idx_heads * tq, LANES), BF16),
                        pltpu.VMEM((heads * tq, 2 * LANES), BF16),
                        pltpu.VMEM((idx_heads, tq), F32),
                        pltpu.VMEM((seq, tq), F32),
                        pltpu.VMEM((1, heads * tq), F32),
                        pltpu.VMEM((1, heads * tq), F32),
                        pltpu.VMEM((LANES, heads * tq), F32)],
        compiler_params=_params(("parallel", "arbitrary")),
        name="dsa",
    )(p1, p1, p2, p2, p2, p2, pos_row, pos_col, _alibi_query_features(heads))


def _tile(n, want):
    t = min(n, want)
    assert n % t == 0, (n, want)
    return t


def _block(x, c, positions, w_ada, b_ada, ln1_g, w_in, q_norm_g, kv_norm_g, w_uq, w_uk, w_uv,
           w_o, ln2_g, w_mlp_in, w_mlp_out, final_g, *, heads, idx_heads, b_heads):
    batch, seq, d = x.shape
    t = batch * seq
    q_lora = w_uq.shape[0]
    kv_lora = w_uk.shape[0]
    d_ff = w_mlp_in.shape[1]
    hw = heads * HEAD_DIM
    iw = idx_heads * HEAD_DIM
    bw = b_heads * HEAD_DIM
    topk = min(TOPK_MAX, seq // 4)
    assert (1 << CHUNK_SHIFT) == CHUNK and QK_ROPE + idx_heads <= LANES

    splits = (hw, HEAD_DIM, HEAD_DIM, iw, HEAD_DIM, idx_heads, q_lora, kv_lora, QK_ROPE)
    offs = [0]
    for s in splits:
        offs.append(offs[-1] + s)
    piece = lambda k, scale=1.0: (offs[k], splits[k], scale)
    pieces = (piece(3), piece(0, HEAD_DIM ** -0.5 * LOG2E), piece(6), piece(7), piece(1),
              piece(2), piece(4), piece(8), piece(5))
    packed_width = -(-offs[-1] // LANES) * LANES
    w_pack = _pack_weights(w_in.T, pieces, packed_width, _tile(d, 256))
    width1 = hw + iw
    kv_off = q_lora + kv_lora

    wq = w_uq.reshape(q_lora, b_heads, HEAD_DIM + QK_ROPE) * ((HEAD_DIM + QK_ROPE) ** -0.5 * LOG2E)
    wqn = wq[:, :, :HEAD_DIM].reshape(q_lora, bw).astype(BF16)
    wqr = jnp.pad(wq[:, :, HEAD_DIM:], ((0, 0), (0, 0), (0, LANES - QK_ROPE))
                  ).reshape(q_lora, b_heads * LANES).astype(BF16)
    wuk = w_uk.astype(BF16)
    wuv = w_uv.astype(BF16)
    wo = w_o.astype(BF16)
    w1 = w_mlp_in.astype(BF16)
    w2 = w_mlp_out.astype(BF16)

    lane = jnp.arange(LANES)
    inv_row = (ROPE_THETA ** (-(2.0 * (lane % (QK_ROPE // 2))).astype(F32) / QK_ROPE)
               ).reshape(1, LANES)
    pos_row = positions.reshape(batch, 1, seq)
    pos_col = positions.reshape(t, 1)
    x2d = x.reshape(t, d)

    mod = _adaln_mod(c, w_ada, b_ada, _tile(6 * d, 512))
    shift1, scale1, gate1, shift2, scale2, gate2 = jnp.split(mod, 6, axis=-1)

    tm_ln = _tile(seq, 512)
    h = _ln_mod(x2d, ln1_g, scale1, shift1, seq, tm_ln)
    p1, p2 = _inproj(h, w_pack, width1, _tile(t, 1024), _tile(math.gcd(width1, w_pack.shape[1] - width1), 1024))

    out_a = _dsa(p1, p2, pos_row, pos_col, batch, seq, heads, idx_heads, topk, kv_off,
                 _tile(seq, 512))

    qn, qr, kn, vb, kr = _mla_proj(p2, pos_col, inv_row, q_norm_g.reshape(1, q_lora),
                                   kv_norm_g.reshape(1, kv_lora), wqn, wqr, wuk, wuv,
                                   q_lora, kv_lora, b_heads, _tile(t, 512))
    out_b = _mla_attention(qn, qr, kn, kr, vb, pos_row, pos_col, batch, seq, b_heads,
                           _tile(seq, 512), _tile(seq, 512), min(b_heads, 16))

    tm = _tile(seq, 1024)
    x1 = _mix_resid(out_a, out_b, wo, x2d, gate1, seq, tm, _tile(d, 1024))
    h2 = _ln_mod(x1, ln2_g, scale2, shift2, seq, tm_ln)
    hid = _mm_relu2(h2, w1, tm, _tile(d_ff, 1024))
    out = _mm_resid_norm(hid, w2, x1, gate2, final_g, seq, _tile(seq, 512), _tile(d_ff, 1024),
                         "mlp_out")
    return out.reshape(batch, seq, d)


def kernel(x, c, positions, w_ada, b_ada, ln1_g, w_in, q_norm_g, kv_norm_g, w_uq, w_uk, w_uv,
           w_o, ln2_g, w_mlp_in, w_mlp_out, final_g):
    assert w_ada.shape[0] == 1, "single-layer block"
    return _block(x, c, positions, w_ada[0], b_ada[0], ln1_g[0], w_in[0], q_norm_g[0],
                  kv_norm_g[0], w_uq[0], w_uk[0], w_uv[0], w_o[0], ln2_g[0], w_mlp_in[0],
                  w_mlp_out[0], final_g, heads=A_HEADS, idx_heads=IDX_HEADS, b_heads=B_HEADS)
```

```python
import functools
import math

import jax
import jax.numpy as jnp
import numpy as np
from jax import lax
from jax.experimental import pallas as pl
from jax.experimental.pallas import tpu as pltpu

F32 = jnp.float32
BF16 = jnp.bfloat16

EPS = 1e-6
CHUNK = 64
CHUNK_SHIFT = 6
A_HEADS = 16
IDX_HEADS = 32
HEAD_DIM = 128
TOPK_MAX = 256
B_HEADS = 16
QK_ROPE = 64
ROPE_THETA = 10000.0

LANES = 128
SUM_COLS = 16
LOG2E = math.log2(math.e)
MASKED = -1e30
INT_MIN = -(2 ** 31)
NEG_INF_KEY = -2139095041
VMEM_LIMIT = 56 * 1024 * 1024
VMEM_LIMIT_WIDE = 60 * 1024 * 1024


def _params(semantics, vmem=VMEM_LIMIT):
    return pltpu.CompilerParams(dimension_semantics=semantics, vmem_limit_bytes=vmem)


def _dot_nt(a, b):
    return lax.dot_general(a, b, (((1,), (1,)), ((), ())), preferred_element_type=F32)


def _dot_tn(a, b):
    return lax.dot_general(a, b, (((0,), (0,)), ((), ())), preferred_element_type=F32)


def _mod_kernel(c_ref, w_ref, b_ref, o_ref):
    c = c_ref[...]
    s = c * (1.0 / (1.0 + jnp.exp(-c)))
    o_ref[...] = jnp.dot(s.astype(BF16), w_ref[...].astype(BF16),
                         preferred_element_type=F32) + b_ref[...]


def _adaln_mod(c, w_ada, b_ada, tn):
    b, d = c.shape
    n = w_ada.shape[1]
    rows = 8
    c_pad = jnp.zeros((rows, d), F32).at[:b].set(c)
    out = pl.pallas_call(
        _mod_kernel,
        grid=(n // tn,),
        in_specs=[pl.BlockSpec((rows, d), lambda j: (0, 0)),
                  pl.BlockSpec((d, tn), lambda j: (0, j)),
                  pl.BlockSpec((1, tn), lambda j: (0, j))],
        out_specs=pl.BlockSpec((rows, tn), lambda j: (0, j)),
        out_shape=jax.ShapeDtypeStruct((rows, n), F32),
        compiler_params=_params(("arbitrary",)),
        name="adaln_mod",
    )(c_pad, w_ada, b_ada.reshape(1, n))
    return out[:b]


def _ln_mod_kernel(x_ref, g_ref, sc_ref, sh_ref, o_ref):
    x = x_ref[...]
    y = x * lax.rsqrt(jnp.mean(x * x, axis=-1, keepdims=True) + EPS) * g_ref[...]
    o_ref[...] = (y * (1.0 + sc_ref[0]) + sh_ref[0]).astype(o_ref.dtype)


def _ln_mod(x2d, g, scale, shift, seq, tm):
    t, d = x2d.shape
    nb = scale.shape[0]
    per = seq // tm
    return pl.pallas_call(
        _ln_mod_kernel,
        grid=(t // tm,),
        in_specs=[pl.BlockSpec((tm, d), lambda i: (i, 0)),
                  pl.BlockSpec((1, d), lambda i: (0, 0)),
                  pl.BlockSpec((1, 1, d), lambda i: (i // per, 0, 0)),
                  pl.BlockSpec((1, 1, d), lambda i: (i // per, 0, 0))],
        out_specs=pl.BlockSpec((tm, d), lambda i: (i, 0)),
        out_shape=jax.ShapeDtypeStruct((t, d), BF16),
        compiler_params=_params(("parallel",)),
        name="ln_mod",
    )(x2d, g.reshape(1, d), scale.reshape(nb, 1, d), shift.reshape(nb, 1, d))


def _pack_kernel(wt_ref, o_ref, *, pieces):
    tr = o_ref.shape[0]
    sub = 4 * LANES

    def emit(rows, dst):
        o_ref[:, dst:dst + rows.shape[0]] = rows.T.astype(o_ref.dtype)

    dst = 0
    pending, pending_rows = [], 0
    for src, width, scale in pieces:
        if width % LANES == 0 and not pending:
            for off in range(0, width, sub):
                n = min(sub, width - off)
                rows = wt_ref[src + off:src + off + n, :]
                emit(rows if scale == 1.0 else rows * scale, dst + off)
            dst += width
            continue
        rows = wt_ref[src:src + width, :]
        pending.append(rows if scale == 1.0 else rows * scale)
        pending_rows += width
        assert pending_rows <= LANES
        if pending_rows == LANES:
            emit(jnp.concatenate(pending, axis=0), dst)
            dst, pending, pending_rows = dst + LANES, [], 0
    if pending:
        pending.append(jnp.zeros((LANES - pending_rows, tr), F32))
        emit(jnp.concatenate(pending, axis=0), dst)
        dst += LANES
    assert dst == o_ref.shape[1]


def _pack_weights(wt, pieces, width, tr):
    ncols, d = wt.shape
    return pl.pallas_call(
        functools.partial(_pack_kernel, pieces=pieces),
        grid=(d // tr,),
        in_specs=[pl.BlockSpec((ncols, tr), lambda i: (0, i))],
        out_specs=pl.BlockSpec((tr, width), lambda i: (i, 0)),
        out_shape=jax.ShapeDtypeStruct((d, width), BF16),
        compiler_params=_params(("parallel",)),
        name="pack_w_in",
    )(wt)


def _inproj_kernel(h_ref, w_ref, o1_ref, o2_ref, *, n1):
    j = pl.program_id(1)
    acc = lambda: jnp.dot(h_ref[...], w_ref[...], preferred_element_type=F32)

    @pl.when(j < n1)
    def _():
        o1_ref[...] = acc().astype(o1_ref.dtype)

    @pl.when(j >= n1)
    def _():
        o2_ref[...] = acc()


def _inproj(h, w, width1, tm, tn):
    t, d = h.shape
    n = w.shape[1]
    n1 = width1 // tn
    return pl.pallas_call(
        functools.partial(_inproj_kernel, n1=n1),
        grid=(t // tm, n // tn),
        in_specs=[pl.BlockSpec((tm, d), lambda i, j: (i, 0)),
                  pl.BlockSpec((d, tn), lambda i, j: (0, j))],
        out_specs=[pl.BlockSpec((tm, tn), lambda i, j: (i, jnp.minimum(j, n1 - 1))),
                   pl.BlockSpec((tm, tn), lambda i, j: (i, jnp.maximum(j - n1, 0)))],
        out_shape=[jax.ShapeDtypeStruct((t, width1), BF16),
                   jax.ShapeDtypeStruct((t, n - width1), F32)],
        compiler_params=_params(("parallel", "arbitrary")),
        name="in_proj",
    )(h, w)


def _mm_relu2_kernel(a_ref, w_ref, o_ref):
    acc = jnp.dot(a_ref[...], w_ref[...], preferred_element_type=F32)
    r = jnp.maximum(acc, 0.0)
    o_ref[...] = (r * r).astype(o_ref.dtype)


def _mm_relu2(a, w, tm, tn):
    t, d = a.shape
    n = w.shape[1]
    return pl.pallas_call(
        _mm_relu2_kernel,
        grid=(t // tm, n // tn),
        in_specs=[pl.BlockSpec((tm, d), lambda i, j: (i, 0)),
                  pl.BlockSpec((d, tn), lambda i, j: (0, j))],
        out_specs=pl.BlockSpec((tm, tn), lambda i, j: (i, j)),
        out_shape=jax.ShapeDtypeStruct((t, n), BF16),
        compiler_params=_params(("parallel", "arbitrary")),
        name="mlp_in",
    )(a, w)


def _mm_resid_kernel(a_ref, w_ref, r_ref, g_ref, fg_ref, o_ref, *, nk):
    k = pl.program_id(1)
    part = lambda: jnp.dot(a_ref[...], w_ref[...], preferred_element_type=F32)

    def finish(acc):
        y = r_ref[...] + g_ref[0] * acc
        o_ref[...] = y * lax.rsqrt(jnp.mean(y * y, axis=-1, keepdims=True) + EPS) * fg_ref[...]

    if nk == 1:
        finish(part())
        return

    @pl.when(k == 0)
    def _():
        o_ref[...] = part()

    @pl.when((k > 0) & (k < nk - 1))
    def _():
        o_ref[...] += part()

    @pl.when(k == nk - 1)
    def _():
        finish(o_ref[...] + part())


def _mm_resid_norm(a, w, resid, gate, norm_g, seq, tm, tk, name):
    t, kdim = a.shape
    n = w.shape[1]
    nb = gate.shape[0]
    per = seq // tm
    nk = kdim // tk
    return pl.pallas_call(
        functools.partial(_mm_resid_kernel, nk=nk),
        grid=(t // tm, nk),
        in_specs=[pl.BlockSpec((tm, tk), lambda i, k: (i, k)),
                  pl.BlockSpec((tk, n), lambda i, k: (k, 0)),
                  pl.BlockSpec((tm, n), lambda i, k: (i, 0)),
                  pl.BlockSpec((1, 1, n), lambda i, k: (i // per, 0, 0)),
                  pl.BlockSpec((1, n), lambda i, k: (0, 0))],
        out_specs=pl.BlockSpec((tm, n), lambda i, k: (i, 0)),
        out_shape=jax.ShapeDtypeStruct((t, n), F32),
        compiler_params=_params(("parallel", "arbitrary"), vmem=VMEM_LIMIT_WIDE),
        name=name,
    )(a, w, resid, gate.reshape(nb, 1, n), norm_g.reshape(1, n))


def _mix_resid_kernel(a1_ref, a2_ref, w1_ref, w2_ref, r_ref, g_ref, o_ref):
    acc = jnp.dot(a1_ref[...], w1_ref[...], preferred_element_type=F32)
    acc += jnp.dot(a2_ref[...], w2_ref[...], preferred_element_type=F32)
    o_ref[...] = r_ref[...] + g_ref[0] * acc


def _mix_resid(a1, a2, w, resid, gate, seq, tm, tn):
    t, k1 = a1.shape
    k2 = a2.shape[1]
    assert k1 == k2
    n = w.shape[1]
    nb = gate.shape[0]
    per = seq // tm
    return pl.pallas_call(
        _mix_resid_kernel,
        grid=(t // tm, n // tn),
        in_specs=[pl.BlockSpec((tm, k1), lambda i, j: (i, 0)),
                  pl.BlockSpec((tm, k2), lambda i, j: (i, 0)),
                  pl.BlockSpec((k1, tn), lambda i, j: (0, j)),
                  pl.BlockSpec((k2, tn), lambda i, j: (1, j)),
                  pl.BlockSpec((tm, tn), lambda i, j: (i, j)),
                  pl.BlockSpec((1, 1, tn), lambda i, j: (i // per, 0, j))],
        out_specs=pl.BlockSpec((tm, tn), lambda i, j: (i, j)),
        out_shape=jax.ShapeDtypeStruct((t, n), F32),
        compiler_params=_params(("parallel", "arbitrary")),
        name="out_proj",
    )(a1, a2, w, w, resid, gate.reshape(nb, 1, n))


def _rope_tables(pos_col, inv_row):
    ang = pos_col.astype(F32) * inv_row
    lane = lax.broadcasted_iota(jnp.int32, ang.shape, 1)
    half = QK_ROPE // 2
    cos = jnp.cos(ang)
    sin = jnp.sin(ang)
    c = jnp.where(lane < QK_ROPE, cos, 0.0)
    s_lo = jnp.where(lane < half, -sin, 0.0)
    s_hi = jnp.where((lane >= half) & (lane < QK_ROPE), sin, 0.0)
    return c, s_lo, s_hi


def _rope_apply(x, c, s_lo, s_hi):
    half = QK_ROPE // 2
    return (x * c + pltpu.roll(x, LANES - half, axis=1) * s_lo
            + pltpu.roll(x, half, axis=1) * s_hi)


def _mla_proj_kernel(cq_ref, ckv_ref, misc_ref, pos_ref, inv_ref, gq_ref, gkv_ref,
                     wqn_ref, wqr_ref, wuk_ref, wuv_ref,
                     qn_ref, qr_ref, kn_ref, vb_ref, kr_ref, *, heads):
    cq = cq_ref[...]
    cqn = (cq * lax.rsqrt(jnp.mean(cq * cq, axis=-1, keepdims=True) + EPS)
           * gq_ref[...]).astype(BF16)
    ckv = ckv_ref[...]
    kvn = (ckv * lax.rsqrt(jnp.mean(ckv * ckv, axis=-1, keepdims=True) + EPS)
           * gkv_ref[...]).astype(BF16)
    c, s_lo, s_hi = _rope_tables(pos_ref[...], inv_ref[...])

    qn_ref[...] = jnp.dot(cqn, wqn_ref[...], preferred_element_type=F32).astype(BF16)
    qr = jnp.dot(cqn, wqr_ref[...], preferred_element_type=F32)
    for h in range(heads):
        sl = slice(h * LANES, (h + 1) * LANES)
        qr_ref[:, sl] = _rope_apply(qr[:, sl], c, s_lo, s_hi).astype(BF16)
    kn_ref[...] = jnp.dot(kvn, wuk_ref[...], preferred_element_type=F32).astype(BF16)
    vb_ref[...] = jnp.dot(kvn, wuv_ref[...], preferred_element_type=F32).astype(BF16)
    kr_ref[...] = _rope_apply(misc_ref[...], c, s_lo, s_hi).astype(BF16)


def _mla_proj(p2, pos_col, inv_row, gq, gkv, wqn, wqr, wuk, wuv, q_lora, kv_lora, heads, tm):
    t = p2.shape[0]
    hw = heads * LANES
    misc_blk = p2.shape[1] // LANES - 1
    const = lambda i: (0, 0)
    return pl.pallas_call(
        functools.partial(_mla_proj_kernel, heads=heads),
        grid=(t // tm,),
        in_specs=[pl.BlockSpec((tm, q_lora), lambda i: (i, 0)),
                  pl.BlockSpec((tm, kv_lora), lambda i: (i, q_lora // kv_lora)),
                  pl.BlockSpec((tm, LANES), lambda i: (i, misc_blk)),
                  pl.BlockSpec((tm, 1), lambda i: (i, 0)),
                  pl.BlockSpec((1, LANES), const),
                  pl.BlockSpec((1, q_lora), const),
                  pl.BlockSpec((1, kv_lora), const),
                  pl.BlockSpec((q_lora, hw), const),
                  pl.BlockSpec((q_lora, hw), const),
                  pl.BlockSpec((kv_lora, hw), const),
                  pl.BlockSpec((kv_lora, hw), const)],
        out_specs=[pl.BlockSpec((tm, hw), lambda i: (i, 0))] * 4
                  + [pl.BlockSpec((tm, LANES), lambda i: (i, 0))],
        out_shape=[jax.ShapeDtypeStruct((t, hw), BF16)] * 4
                  + [jax.ShapeDtypeStruct((t, LANES), BF16)],
        compiler_params=_params(("parallel",)),
        name="mla_proj",
    )(p2, p2, p2, pos_col, inv_row, gq, gkv, wqn, wqr, wuk, wuv)


def _mla_attn_kernel(it_ref, jt_ref, qn_ref, qr_ref, kn_ref, kr_ref, v_ref, qpos_ref, kpos_ref,
                     o_ref, m_ref, l_ref, acc_ref, *, group, tq, tk):
    p = pl.program_id(2)
    i = it_ref[p]
    j = jt_ref[p]

    @pl.when(j == 0)
    def _():
        m_ref[...] = jnp.full(m_ref.shape, MASKED, F32)
        l_ref[...] = jnp.zeros(l_ref.shape, F32)
        acc_ref[...] = jnp.zeros(acc_ref.shape, F32)

    def step(masked):
        kr = kr_ref[...]
        if masked:
            adm = (kpos_ref[...] >> CHUNK_SHIFT) <= (qpos_ref[0] >> CHUNK_SHIFT)
        def scores(g):
            sl = slice(g * LANES, (g + 1) * LANES)
            q2 = jnp.concatenate([qn_ref[:, sl], qr_ref[:, sl]], axis=1)
            k2 = jnp.concatenate([kn_ref[:, sl], kr], axis=1)
            return _dot_nt(k2, q2)

        depth = 4
        ahead = [scores(g) for g in range(min(depth, group))]
        for g in range(group):
            sl = slice(g * LANES, (g + 1) * LANES)
            st = ahead.pop(0)
            if g + depth < group:
                ahead.append(scores(g + depth))
            if masked:
                st = jnp.where(adm, st, MASKED)
            m_old = m_ref[g:g + 1, :]
            m_new = jnp.maximum(m_old, jnp.max(st, axis=0, keepdims=True))
            alpha = jnp.exp2(m_old - m_new)
            pt = jnp.exp2(st - m_new)
            l_ref[g:g + 1, :] = alpha * l_ref[g:g + 1, :] + jnp.sum(pt, axis=0, keepdims=True)
            m_ref[g:g + 1, :] = m_new
            acc_ref[sl, :] = alpha * acc_ref[sl, :] + _dot_tn(v_ref[:, sl], pt.astype(BF16))

    reaches = (j + 1) * tk > i * tq

    @pl.when(reaches)
    def _():
        step(True)

    @pl.when(jnp.logical_not(reaches))
    def _():
        step(False)

    @pl.when(j == ((i + 1) * tq - 1) // tk)
    def _():
        for g in range(group):
            sl = slice(g * LANES, (g + 1) * LANES)
            o_ref[:, sl] = (acc_ref[sl, :] / l_ref[g:g + 1, :]).T.astype(o_ref.dtype)


def _mla_attention(qn, qr, kn, kr, vb, pos_row, pos_col, batch, seq, heads, tq, tk, group):
    nq = seq // tq
    nk = seq // tk
    gw = group * LANES
    pairs = [(i, j) for i in range(nq) for j in range(nk) if j * tk < (i + 1) * tq]
    it = jnp.asarray([p[0] for p in pairs], jnp.int32)
    jt = jnp.asarray([p[1] for p in pairs], jnp.int32)
    qmap = lambda b, h, p, it, jt: (b * nq + it[p], h)
    kmap = lambda b, h, p, it, jt: (b * nk + jt[p], h)
    grid_spec = pltpu.PrefetchScalarGridSpec(
        num_scalar_prefetch=2,
        grid=(batch, heads // group, len(pairs)),
        in_specs=[pl.BlockSpec((tq, gw), qmap),
                  pl.BlockSpec((tq, gw), qmap),
                  pl.BlockSpec((tk, gw), kmap),
                  pl.BlockSpec((tk, LANES), lambda b, h, p, it, jt: (b * nk + jt[p], 0)),
                  pl.BlockSpec((tk, gw), kmap),
                  pl.BlockSpec((1, 1, tq), lambda b, h, p, it, jt: (b, 0, it[p])),
                  pl.BlockSpec((tk, 1), lambda b, h, p, it, jt: (b * nk + jt[p], 0))],
        out_specs=pl.BlockSpec((tq, gw), qmap),
        scratch_shapes=[pltpu.VMEM((group, tq), F32), pltpu.VMEM((group, tq), F32),
                        pltpu.VMEM((gw, tq), F32)],
    )
    return pl.pallas_call(
        functools.partial(_mla_attn_kernel, group=group, tq=tq, tk=tk),
        grid_spec=grid_spec,
        out_shape=jax.ShapeDtypeStruct((batch * seq, heads * LANES), BF16),
        compiler_params=_params(("parallel", "parallel", "arbitrary")),
        name="mla_attn",
    )(it, jt, qn, qr, kn, kr, vb, pos_row, pos_col)


def _tree_sum8(x):
    parts = [x[r:r + 8] for r in range(0, x.shape[0], 8)]
    while len(parts) > 1:
        nxt = [parts[a] + parts[a + 1] for a in range(0, len(parts) - 1, 2)]
        if len(parts) % 2:
            nxt.append(parts[-1])
        parts = nxt
    return parts[0]


def _dsa_kernel(qi_ref, qa_ref, misc_ref, kidx_ref, ka_ref, va_ref, qpos_ref, kpos_ref,
                slopef_ref, o_ref,
                qis_ref, qas_ref, wt_ref, sc_ref, m_ref, l_ref, acc_ref,
                *, tq, kc, idx_heads, heads, topk, hg):
    i = pl.program_id(1)
    nkc = ((i + 1) * tq + kc - 1) // kc

    for h in range(idx_heads):
        qis_ref[h * tq:(h + 1) * tq, :] = qi_ref[:, h * LANES:(h + 1) * LANES]
    for h in range(heads):
        qas_ref[h * tq:(h + 1) * tq, :LANES] = qa_ref[:, h * LANES:(h + 1) * LANES]
    wt_ref[...] = misc_ref[...].T[QK_ROPE:QK_ROPE + idx_heads, :] * (
        (idx_heads ** -0.5) * (HEAD_DIM ** -0.5))
    qpos = qpos_ref[0]
    qchunk = qpos >> CHUNK_SHIFT

    def score_chunk(c, carry):
        r0 = pl.multiple_of(c * kc, kc)
        kx = kidx_ref[pl.ds(r0, kc), :].astype(BF16)
        score = jnp.zeros((kc, tq), F32)
        for g in range(idx_heads // hg):
            lt = _dot_nt(kx, qis_ref[g * hg * tq:(g + 1) * hg * tq, :])
            for hh in range(hg):
                h = g * hg + hh
                score += jnp.maximum(lt[:, hh * tq:(hh + 1) * tq], 0.0) * wt_ref[h:h + 1, :]
        adm = (kpos_ref[pl.ds(r0, kc), :] >> CHUNK_SHIFT) <= qchunk
        sc_ref[pl.ds(r0, kc), :] = jnp.where(adm, score, -jnp.inf)
        return carry

    lax.fori_loop(0, nkc, score_chunk, 0)

    def key_to_float(key):
        return pltpu.bitcast(key ^ ((key >> 31) & 0x7FFFFFFF), F32)

    def count_ge(thr):
        def body(c, cnt):
            r0 = pl.multiple_of(c * kc, kc)
            hit = jnp.where(sc_ref[pl.ds(r0, kc), :] >= thr, 1.0, 0.0)
            return cnt + _tree_sum8(hit)
        cnt8 = lax.fori_loop(0, nkc, body, jnp.zeros((8, tq), F32))
        return jnp.sum(cnt8, axis=0, keepdims=True)

    need = float(topk)
    key = jnp.where(count_ge(jnp.zeros((1, tq), F32)) >= need, 0, INT_MIN)

    def bit_step(it, key):
        cand = key + lax.shift_left(jnp.int32(1), 30 - it)
        return jnp.where(count_ge(key_to_float(cand)) >= need, cand, key)

    key = lax.fori_loop(0, 31, bit_step, key)
    key = jnp.maximum(key, NEG_INF_KEY + 1)
    thr = key_to_float(key)

    @pl.when(jnp.max(count_ge(thr)) > need)
    def _():
        keep = need - count_ge(key_to_float(key + 1))

        def ties_below(bound):
            def body(c, cnt):
                r0 = pl.multiple_of(c * kc, kc)
                idx = r0 + lax.broadcasted_iota(jnp.int32, (kc, 1), 0)
                tie = sc_ref[pl.ds(r0, kc), :] == thr
                hit = jnp.where(tie, jnp.where(idx < bound, 1.0, 0.0), 0.0)
                return cnt + _tree_sum8(hit)
            cnt8 = lax.fori_loop(0, nkc, body, jnp.zeros((8, tq), F32))
            return jnp.sum(cnt8, axis=0, keepdims=True)

        nbits = sc_ref.shape[0].bit_length()

        def bound_step(it, below):
            cand = below + lax.shift_left(jnp.int32(1), nbits - 1 - it)
            return jnp.where(ties_below(cand) < keep, cand, below)

        bound = lax.fori_loop(0, nbits, bound_step, jnp.zeros((1, tq), jnp.int32)) + 1

        def demote(c, carry):
            r0 = pl.multiple_of(c * kc, kc)
            idx = r0 + lax.broadcasted_iota(jnp.int32, (kc, 1), 0)
            sc = sc_ref[pl.ds(r0, kc), :]
            sc_ref[pl.ds(r0, kc), :] = jnp.where(
                sc == thr, jnp.where(idx >= bound, -jnp.inf, sc), sc)
            return carry

        lax.fori_loop(0, nkc, demote, 0)

    m_ref[...] = jnp.full(m_ref.shape, MASKED, F32)
    l_ref[...] = jnp.zeros(l_ref.shape, F32)
    acc_ref[...] = jnp.zeros(acc_ref.shape, F32)

    for h in range(heads):
        qas_ref[h * tq:(h + 1) * tq, LANES:] = jnp.broadcast_to(
            slopef_ref[h:h + 1, :], (tq, LANES)).astype(BF16)
    qpos0 = qpos[:, 0:1]
    lane = lax.broadcasted_iota(jnp.int32, (1, LANES), 1)

    def softmax_update(st, sel, vx, bias):
        m_all = m_ref[...]
        mask_bias = jnp.where(sel, 0.0, MASKED)
        ps, ms = [], []
        for h in range(heads):
            sl = slice(h * tq, (h + 1) * tq)
            s = st[:, sl] if bias is None else st[:, sl] - bias(h)
            s = s + mask_bias
            m_new = jnp.maximum(m_all[:, sl], jnp.max(s, axis=0, keepdims=True))
            ms.append(m_new)
            ps.append(jnp.exp2((s - m_new).astype(BF16)))
        m_new = jnp.concatenate(ms, axis=1)
        alpha = jnp.exp2(m_all - m_new)
        m_ref[...] = m_new
        vx1 = jnp.concatenate([vx, jnp.ones((vx.shape[0], SUM_COLS), BF16)], axis=1)
        pv = _dot_tn(vx1, jnp.concatenate(ps, axis=1))
        l_ref[...] = alpha * l_ref[...] + pv[LANES:LANES + 1, :]
        acc_ref[...] = alpha * acc_ref[...] + pv[:LANES, :]

    def attend_chunk(c, carry):
        r0 = pl.multiple_of(c * kc, kc)
        rel = kpos_ref[pl.ds(r0, kc), :] - qpos0
        feat = jnp.where(lane < 3, (rel >> CHUNK_SHIFT).astype(F32),
                         jnp.where(lane < 6, (rel & (CHUNK - 1)).astype(F32), 0.0))
        kx = jnp.concatenate([ka_ref[pl.ds(r0, kc), :].astype(BF16), feat.astype(BF16)], axis=1)
        vx = va_ref[pl.ds(r0, kc), :].astype(BF16)
        sel = sc_ref[pl.ds(r0, kc), :] >= jnp.where(rel < 0, thr, jnp.inf)
        softmax_update(_dot_nt(kx, qas_ref[...]), sel, vx, None)
        return carry

    lax.fori_loop(0, (i * tq + kc - 1) // kc, attend_chunk, 0)

    r0 = pl.multiple_of(i * tq, tq)
    kp = kpos_ref[pl.ds(r0, tq), :]
    shifted = (jnp.abs(kp - qpos) - (qpos - qpos0)).astype(F32)
    sel = sc_ref[pl.ds(r0, tq), :] >= thr
    st = _dot_nt(ka_ref[pl.ds(r0, tq), :].astype(BF16), qas_ref[:, :LANES])
    softmax_update(st, sel, va_ref[pl.ds(r0, tq), :].astype(BF16),
                   lambda h: (LOG2E * 2.0 ** (-8.0 * (h + 1) / heads)) * shifted)

    for h in range(heads):
        sl = slice(h * tq, (h + 1) * tq)
        o_ref[:, h * LANES:(h + 1) * LANES] = (acc_ref[:, sl] / l_ref[:, sl]).T.astype(o_ref.dtype)


def _alibi_query_features(heads):
    rows = np.zeros((heads, LANES), np.float32)
    for h in range(heads):
        rest = LOG2E * 2.0 ** (-8.0 * (h + 1) / heads)
        for k in range(3):
            piece = float(np.asarray(rest, np.float32).astype(BF16).astype(np.float32))
            rows[h, k] = CHUNK * piece
            rows[h, 3 + k] = piece
            rest -= piece
    return jnp.asarray(rows)


def _dsa(p1, p2, pos_row, pos_col, batch, seq, heads, idx_heads, topk, kv_off, kc):
    tq = LANES
    nq = seq // tq
    hw = heads * LANES
    iw = idx_heads * LANES
    assert iw % hw == 0
    kblk = kv_off // LANES
    misc_blk = p2.shape[1] // LANES - 1
    hg = min(8, idx_heads)
    kernel = functools.partial(_dsa_kernel, tq=tq, kc=kc, idx_heads=idx_heads, heads=heads,
                               topk=topk, hg=hg)
    return pl.pallas_call(
        kernel,
        grid=(batch, nq),
        in_specs=[pl.BlockSpec((tq, iw), lambda b, i: (b * nq + i, 0)),
                  pl.BlockSpec((tq, hw), lambda b, i: (b * nq + i, iw // hw)),
                  pl.BlockSpec((tq, LANES), lambda b, i: (b * nq + i, misc_blk)),
                  pl.BlockSpec((seq, LANES), lambda b, i: (b, kblk + 2)),
                  pl.BlockSpec((seq, LANES), lambda b, i: (b, kblk)),
                  pl.BlockSpec((seq, LANES), lambda b, i: (b, kblk + 1)),
                  pl.BlockSpec((1, 1, tq), lambda b, i: (b, 0, i)),
                  pl.BlockSpec((seq, 1), lambda b, i: (b, 0)),
                  pl.BlockSpec((heads, LANES), lambda b, i: (0, 0))],
        out_specs=pl.BlockSpec((tq, hw), lambda b, i: (b * nq + i, 0)),
        out_shape=jax.ShapeDtypeStruct((batch * seq, hw), BF16),
        scratch_shapes=[pltpu.VMEM((idx_heads * tq, LANES), BF16),
                        pltpu.VMEM((heads * tq, 2 * LANES), BF16),
                        pltpu.VMEM((idx_heads, tq), F32),
                        pltpu.VMEM((seq, tq), F32),
                        pltpu.VMEM((1, heads * tq), F32),
                        pltpu.VMEM((1, heads * tq), F32),
                        pltpu.VMEM((LANES, heads * tq), F32)],
        compiler_params=_params(("parallel", "arbitrary")),
        name="dsa",
    )(p1, p1, p2, p2, p2, p2, pos_row, pos_col, _alibi_query_features(heads))


def _tile(n, want):
    t = min(n, want)
    assert n % t == 0, (n, want)
    return t


def _block(x, c, positions, w_ada, b_ada, ln1_g, w_in, q_norm_g, kv_norm_g, w_uq, w_uk, w_uv,
           w_o, ln2_g, w_mlp_in, w_mlp_out, final_g, *, heads, idx_heads, b_heads):
    batch, seq, d = x.shape
    t = batch * seq
    q_lora = w_uq.shape[0]
    kv_lora = w_uk.shape[0]
    d_ff = w_mlp_in.shape[1]
    hw = heads * HEAD_DIM
    iw = idx_heads * HEAD_DIM
    bw = b_heads * HEAD_DIM
    topk = min(TOPK_MAX, seq // 4)
    assert (1 << CHUNK_SHIFT) == CHUNK and QK_ROPE + idx_heads <= LANES

    splits = (hw, HEAD_DIM, HEAD_DIM, iw, HEAD_DIM, idx_heads, q_lora, kv_lora, QK_ROPE)
    offs = [0]
    for s in splits:
        offs.append(offs[-1] + s)
    piece = lambda k, scale=1.0: (offs[k], splits[k], scale)
    pieces = (piece(3), piece(0, HEAD_DIM ** -0.5 * LOG2E), piece(6), piece(7), piece(1),
              piece(2), piece(4), piece(8), piece(5))
    packed_width = -(-offs[-1] // LANES) * LANES
    w_pack = _pack_weights(w_in.T, pieces, packed_width, _tile(d, 256))
    width1 = hw + iw
    kv_off = q_lora + kv_lora

    wq = w_uq.reshape(q_lora, b_heads, HEAD_DIM + QK_ROPE) * ((HEAD_DIM + QK_ROPE) ** -0.5 * LOG2E)
    wqn = wq[:, :, :HEAD_DIM].reshape(q_lora, bw).astype(BF16)
    wqr = jnp.pad(wq[:, :, HEAD_DIM:], ((0, 0), (0, 0), (0, LANES - QK_ROPE))
                  ).reshape(q_lora, b_heads * LANES).astype(BF16)
    wuk = w_uk.astype(BF16)
    wuv = w_uv.astype(BF16)
    wo = w_o.astype(BF16)
    w1 = w_mlp_in.astype(BF16)
    w2 = w_mlp_out.astype(BF16)

    lane = jnp.arange(LANES)
    inv_row = (ROPE_THETA ** (-(2.0 * (lane % (QK_ROPE // 2))).astype(F32) / QK_ROPE)
               ).reshape(1, LANES)
    pos_row = positions.reshape(batch, 1, seq)
    pos_col = positions.reshape(t, 1)
    x2d = x.reshape(t, d)

    mod = _adaln_mod(c, w_ada, b_ada, _tile(6 * d, 512))
    shift1, scale1, gate1, shift2, scale2, gate2 = jnp.split(mod, 6, axis=-1)

    tm_ln = _tile(seq, 512)
    h = _ln_mod(x2d, ln1_g, scale1, shift1, seq, tm_ln)
    p1, p2 = _inproj(h, w_pack, width1, _tile(t, 1024), _tile(math.gcd(width1, w_pack.shape[1] - width1), 1024))

    out_a = _dsa(p1, p2, pos_row, pos_col, batch, seq, heads, idx_heads, topk, kv_off,
                 _tile(seq, 512))

    qn, qr, kn, vb, kr = _mla_proj(p2, pos_col, inv_row, q_norm_g.reshape(1, q_lora),
                                   kv_norm_g.reshape(1, kv_lora), wqn, wqr, wuk, wuv,
                                   q_lora, kv_lora, b_heads, _tile(t, 512))
    out_b = _mla_attention(qn, qr, kn, kr, vb, pos_row, pos_col, batch, seq, b_heads,
                           _tile(seq, 512), _tile(seq, 512), min(b_heads, 16))

    tm = _tile(seq, 1024)
    x1 = _mix_resid(out_a, out_b, wo, x2d, gate1, seq, tm, _tile(d, 1024))
    h2 = _ln_mod(x1, ln2_g, scale2, shift2, seq, tm_ln)
    hid = _mm_relu2(h2, w1, tm, _tile(d_ff, 1024))
    out = _mm_resid_norm(hid, w2, x1, gate2, final_g, seq, _tile(seq, 512), _tile(d_ff, 1024),
                         "mlp_out")
    return out.reshape(batch, seq, d)


def kernel(x, c, positions, w_ada, b_ada, ln1_g, w_in, q_norm_g, kv_norm_g, w_uq, w_uk, w_uv,
           w_o, ln2_g, w_mlp_in, w_mlp_out, final_g):
    assert w_ada.shape[0] == 1, "single-layer block"
    return _block(x, c, positions, w_ada[0], b_ada[0], ln1_g[0], w_in[0], q_norm_g[0],
                  kv_norm_g[0], w_uq[0], w_uk[0], w_uv[0], w_o[0], ln2_g[0], w_mlp_in[0],
                  w_mlp_out[0], final_g, heads=A_HEADS, idx_heads=IDX_HEADS, b_heads=B_HEADS)
```

```python
import functools
import math

import jax
import jax.numpy as jnp
import numpy as np
from jax import lax
from jax.experimental import pallas as pl
from jax.experimental.pallas import tpu as pltpu

F32 = jnp.float32
BF16 = jnp.bfloat16

EPS = 1e-6
CHUNK = 64
CHUNK_SHIFT = 6
A_HEADS = 16
IDX_HEADS = 32
HEAD_DIM = 128
TOPK_MAX = 256
B_HEADS = 16
QK_ROPE = 64
ROPE_THETA = 10000.0

LANES = 128
SUM_COLS = 16
LOG2E = math.log2(math.e)
MASKED = -1e30
INT_MIN = -(2 ** 31)
NEG_INF_KEY = -2139095041
VMEM_LIMIT = 56 * 1024 * 1024
VMEM_LIMIT_WIDE = 60 * 1024 * 1024


def _params(semantics, vmem=VMEM_LIMIT):
    return pltpu.CompilerParams(dimension_semantics=semantics, vmem_limit_bytes=vmem)


def _dot_nt(a, b):
    return lax.dot_general(a, b, (((1,), (1,)), ((), ())), preferred_element_type=F32)


def _dot_tn(a, b):
    return lax.dot_general(a, b, (((0,), (0,)), ((), ())), preferred_element_type=F32)


def _mod_kernel(c_ref, w_ref, b_ref, o_ref):
    c = c_ref[...]
    s = c * (1.0 / (1.0 + jnp.exp(-c)))
    o_ref[...] = jnp.dot(s.astype(BF16), w_ref[...].astype(BF16),
                         preferred_element_type=F32) + b_ref[...]


def _adaln_mod(c, w_ada, b_ada, tn):
    b, d = c.shape
    n = w_ada.shape[1]
    rows = 8
    c_pad = jnp.zeros((rows, d), F32).at[:b].set(c)
    out = pl.pallas_call(
        _mod_kernel,
        grid=(n // tn,),
        in_specs=[pl.BlockSpec((rows, d), lambda j: (0, 0)),
                  pl.BlockSpec((d, tn), lambda j: (0, j)),
                  pl.BlockSpec((1, tn), lambda j: (0, j))],
        out_specs=pl.BlockSpec((rows, tn), lambda j: (0, j)),
        out_shape=jax.ShapeDtypeStruct((rows, n), F32),
        compiler_params=_params(("arbitrary",)),
        name="adaln_mod",
    )(c_pad, w_ada, b_ada.reshape(1, n))
    return out[:b]


def _ln_mod_kernel(x_ref, g_ref, sc_ref, sh_ref, o_ref):
    x = x_ref[...]
    y = x * lax.rsqrt(jnp.mean(x * x, axis=-1, keepdims=True) + EPS) * g_ref[...]
    o_ref[...] = (y * (1.0 + sc_ref[0]) + sh_ref[0]).astype(o_ref.dtype)


def _ln_mod(x2d, g, scale, shift, seq, tm):
    t, d = x2d.shape
    nb = scale.shape[0]
    per = seq // tm
    return pl.pallas_call(
        _ln_mod_kernel,
        grid=(t // tm,),
        in_specs=[pl.BlockSpec((tm, d), lambda i: (i, 0)),
                  pl.BlockSpec((1, d), lambda i: (0, 0)),
                  pl.BlockSpec((1, 1, d), lambda i: (i // per, 0, 0)),
                  pl.BlockSpec((1, 1, d), lambda i: (i // per, 0, 0))],
        out_specs=pl.BlockSpec((tm, d), lambda i: (i, 0)),
        out_shape=jax.ShapeDtypeStruct((t, d), BF16),
        compiler_params=_params(("parallel",)),
        name="ln_mod",
    )(x2d, g.reshape(1, d), scale.reshape(nb, 1, d), shift.reshape(nb, 1, d))


def _pack_kernel(wt_ref, o_ref, *, pieces):
    tr = o_ref.shape[0]
    sub = 4 * LANES

    def emit(rows, dst):
        o_ref[:, dst:dst + rows.shape[0]] = rows.T.astype(o_ref.dtype)

    dst = 0
    pending, pending_rows = [], 0
    for src, width, scale in pieces:
        if width % LANES == 0 and not pending:
            for off in range(0, width, sub):
                n = min(sub, width - off)
                rows = wt_ref[src + off:src + off + n, :]
                emit(rows if scale == 1.0 else rows * scale, dst + off)
            dst += width
            continue
        rows = wt_ref[src:src + width, :]
        pending.append(rows if scale == 1.0 else rows * scale)
        pending_rows += width
        assert pending_rows <= LANES
        if pending_rows == LANES:
            emit(jnp.concatenate(pending, axis=0), dst)
            dst, pending, pending_rows = dst + LANES, [], 0
    if pending:
        pending.append(jnp.zeros((LANES - pending_rows, tr), F32))
        emit(jnp.concatenate(pending, axis=0), dst)
        dst += LANES
    assert dst == o_ref.shape[1]


def _pack_weights(wt, pieces, width, tr):
    ncols, d = wt.shape
    return pl.pallas_call(
        functools.partial(_pack_kernel, pieces=pieces),
        grid=(d // tr,),
        in_specs=[pl.BlockSpec((ncols, tr), lambda i: (0, i))],
        out_specs=pl.BlockSpec((tr, width), lambda i: (i, 0)),
        out_shape=jax.ShapeDtypeStruct((d, width), BF16),
        compiler_params=_params(("parallel",)),
        name="pack_w_in",
    )(wt)


def _inproj_kernel(h_ref, w_ref, o1_ref, o2_ref, *, n1):
    j = pl.program_id(1)
    acc = lambda: jnp.dot(h_ref[...], w_ref[...], preferred_element_type=F32)

    @pl.when(j < n1)
    def _():
        o1_ref[...] = acc().astype(o1_ref.dtype)

    @pl.when(j >= n1)
    def _():
        o2_ref[...] = acc()


def _inproj(h, w, width1, tm, tn):
    t, d = h.shape
    n = w.shape[1]
    n1 = width1 // tn
    return pl.pallas_call(
        functools.partial(_inproj_kernel, n1=n1),
        grid=(t // tm, n // tn),
        in_specs=[pl.BlockSpec((tm, d), lambda i, j: (i, 0)),
                  pl.BlockSpec((d, tn), lambda i, j: (0, j))],
        out_specs=[pl.BlockSpec((tm, tn), lambda i, j: (i, jnp.minimum(j, n1 - 1))),
                   pl.BlockSpec((tm, tn), lambda i, j: (i, jnp.maximum(j - n1, 0)))],
        out_shape=[jax.ShapeDtypeStruct((t, width1), BF16),
                   jax.ShapeDtypeStruct((t, n - width1), F32)],
        compiler_params=_params(("parallel", "arbitrary")),
        name="in_proj",
    )(h, w)


def _mm_relu2_kernel(a_ref, w_ref, o_ref):
    acc = jnp.dot(a_ref[...], w_ref[...], preferred_element_type=F32)
    r = jnp.maximum(acc, 0.0)
    o_ref[...] = (r * r).astype(o_ref.dtype)


def _mm_relu2(a, w, tm, tn):
    t, d = a.shape
    n = w.shape[1]
    return pl.pallas_call(
        _mm_relu2_kernel,
        grid=(t // tm, n // tn),
        in_specs=[pl.BlockSpec((tm, d), lambda i, j: (i, 0)),
                  pl.BlockSpec((d, tn), lambda i, j: (0, j))],
        out_specs=pl.BlockSpec((tm, tn), lambda i, j: (i, j)),
        out_shape=jax.ShapeDtypeStruct((t, n), BF16),
        compiler_params=_params(("parallel", "arbitrary")),
        name="mlp_in",
    )(a, w)


def _mm_resid_kernel(a_ref, w_ref, r_hbm, g_ref, fg_ref, o_ref, rbuf_ref, rsem, *, nk):
    i = pl.program_id(0)
    k = pl.program_id(1)
    tm = rbuf_ref.shape[0]
    part = lambda: jnp.dot(a_ref[...], w_ref[...], preferred_element_type=F32)

    def resid_copy():
        rows = pl.ds(pl.multiple_of(i * tm, tm), tm)
        return pltpu.make_async_copy(r_hbm.at[rows, :], rbuf_ref, rsem)

    def finish(acc):
        resid_copy().wait()
        y = rbuf_ref[...] + g_ref[0] * acc
        o_ref[...] = y * lax.rsqrt(jnp.mean(y * y, axis=-1, keepdims=True) + EPS) * fg_ref[...]

    if nk == 1:
        resid_copy().start()
        finish(part())
        return

    @pl.when(k == 0)
    def _():
        resid_copy().start()
        o_ref[...] = part()

    @pl.when((k > 0) & (k < nk - 1))
    def _():
        o_ref[...] += part()

    @pl.when(k == nk - 1)
    def _():
        finish(o_ref[...] + part())


def _mm_resid_norm(a, w, resid, gate, norm_g, seq, tm, tk, name):
    t, kdim = a.shape
    n = w.shape[1]
    nb = gate.shape[0]
    per = seq // tm
    nk = kdim // tk
    return pl.pallas_call(
        functools.partial(_mm_resid_kernel, nk=nk),
        grid=(t // tm, nk),
        in_specs=[pl.BlockSpec((tm, tk), lambda i, k: (i, k)),
                  pl.BlockSpec((tk, n), lambda i, k: (k, 0)),
                  pl.BlockSpec(memory_space=pl.ANY),
                  pl.BlockSpec((1, 1, n), lambda i, k: (i // per, 0, 0)),
                  pl.BlockSpec((1, n), lambda i, k: (0, 0))],
        out_specs=pl.BlockSpec((tm, n), lambda i, k: (i, 0)),
        out_shape=jax.ShapeDtypeStruct((t, n), F32),
        scratch_shapes=[pltpu.VMEM((tm, n), F32), pltpu.SemaphoreType.DMA(())],
        compiler_params=_params(("arbitrary", "arbitrary"), vmem=VMEM_LIMIT_WIDE),
        name=name,
    )(a, w, resid, gate.reshape(nb, 1, n), norm_g.reshape(1, n))


def _mix_resid_kernel(a1_ref, a2_ref, w1_ref, w2_ref, r_ref, g_ref, o_ref):
    acc = jnp.dot(a1_ref[...], w1_ref[...], preferred_element_type=F32)
    acc += jnp.dot(a2_ref[...], w2_ref[...], preferred_element_type=F32)
    o_ref[...] = r_ref[...] + g_ref[0] * acc


def _mix_resid(a1, a2, w, resid, gate, seq, tm, tn):
    t, k1 = a1.shape
    k2 = a2.shape[1]
    assert k1 == k2
    n = w.shape[1]
    nb = gate.shape[0]
    per = seq // tm
    return pl.pallas_call(
        _mix_resid_kernel,
        grid=(t // tm, n // tn),
        in_specs=[pl.BlockSpec((tm, k1), lambda i, j: (i, 0)),
                  pl.BlockSpec((tm, k2), lambda i, j: (i, 0)),
                  pl.BlockSpec((k1, tn), lambda i, j: (0, j)),
                  pl.BlockSpec((k2, tn), lambda i, j: (1, j)),
                  pl.BlockSpec((tm, tn), lambda i, j: (i, j)),
                  pl.BlockSpec((1, 1, tn), lambda i, j: (i // per, 0, j))],
        out_specs=pl.BlockSpec((tm, tn), lambda i, j: (i, j)),
        out_shape=jax.ShapeDtypeStruct((t, n), F32),
        compiler_params=_params(("parallel", "arbitrary")),
        name="out_proj",
    )(a1, a2, w, w, resid, gate.reshape(nb, 1, n))


def _rope_tables(pos_col, inv_row):
    ang = pos_col.astype(F32) * inv_row
    lane = lax.broadcasted_iota(jnp.int32, ang.shape, 1)
    half = QK_ROPE // 2
    cos = jnp.cos(ang)
    sin = jnp.sin(ang)
    c = jnp.where(lane < QK_ROPE, cos, 0.0)
    s_lo = jnp.where(lane < half, -sin, 0.0)
    s_hi = jnp.where((lane >= half) & (lane < QK_ROPE), sin, 0.0)
    return c, s_lo, s_hi


def _rope_apply(x, c, s_lo, s_hi):
    half = QK_ROPE // 2
    return (x * c + pltpu.roll(x, LANES - half, axis=1) * s_lo
            + pltpu.roll(x, half, axis=1) * s_hi)


def _mla_proj_kernel(cq_ref, ckv_ref, misc_ref, pos_ref, inv_ref, gq_ref, gkv_ref,
                     wqn_ref, wqr_ref, wuk_ref, wuv_ref,
                     qn_ref, qr_ref, kn_ref, vb_ref, kr_ref, *, heads):
    cq = cq_ref[...]
    cqn = (cq * lax.rsqrt(jnp.mean(cq * cq, axis=-1, keepdims=True) + EPS)
           * gq_ref[...]).astype(BF16)
    ckv = ckv_ref[...]
    kvn = (ckv * lax.rsqrt(jnp.mean(ckv * ckv, axis=-1, keepdims=True) + EPS)
           * gkv_ref[...]).astype(BF16)
    c, s_lo, s_hi = _rope_tables(pos_ref[...], inv_ref[...])

    qn_ref[...] = jnp.dot(cqn, wqn_ref[...], preferred_element_type=F32).astype(BF16)
    qr = jnp.dot(cqn, wqr_ref[...], preferred_element_type=F32)
    for h in range(heads):
        sl = slice(h * LANES, (h + 1) * LANES)
        qr_ref[:, sl] = _rope_apply(qr[:, sl], c, s_lo, s_hi).astype(BF16)
    kn_ref[...] = jnp.dot(kvn, wuk_ref[...], preferred_element_type=F32).astype(BF16)
    vb_ref[...] = jnp.dot(kvn, wuv_ref[...], preferred_element_type=F32).astype(BF16)
    kr_ref[...] = _rope_apply(misc_ref[...], c, s_lo, s_hi).astype(BF16)


def _mla_proj(p2, pos_col, inv_row, gq, gkv, wqn, wqr, wuk, wuv, q_lora, kv_lora, heads, tm):
    t = p2.shape[0]
    hw = heads * LANES
    misc_blk = p2.shape[1] // LANES - 1
    const = lambda i: (0, 0)
    return pl.pallas_call(
        functools.partial(_mla_proj_kernel, heads=heads),
        grid=(t // tm,),
        in_specs=[pl.BlockSpec((tm, q_lora), lambda i: (i, 0)),
                  pl.BlockSpec((tm, kv_lora), lambda i: (i, q_lora // kv_lora)),
                  pl.BlockSpec((tm, LANES), lambda i: (i, misc_blk)),
                  pl.BlockSpec((tm, 1), lambda i: (i, 0)),
                  pl.BlockSpec((1, LANES), const),
                  pl.BlockSpec((1, q_lora), const),
                  pl.BlockSpec((1, kv_lora), const),
                  pl.BlockSpec((q_lora, hw), const),
                  pl.BlockSpec((q_lora, hw), const),
                  pl.BlockSpec((kv_lora, hw), const),
                  pl.BlockSpec((kv_lora, hw), const)],
        out_specs=[pl.BlockSpec((tm, hw), lambda i: (i, 0))] * 4
                  + [pl.BlockSpec((tm, LANES), lambda i: (i, 0))],
        out_shape=[jax.ShapeDtypeStruct((t, hw), BF16)] * 4
                  + [jax.ShapeDtypeStruct((t, LANES), BF16)],
        compiler_params=_params(("parallel",)),
        name="mla_proj",
    )(p2, p2, p2, pos_col, inv_row, gq, gkv, wqn, wqr, wuk, wuv)


def _mla_attn_kernel(it_ref, jt_ref, qn_ref, qr_ref, kn_ref, kr_ref, v_ref, qpos_ref, kpos_ref,
                     o_ref, m_ref, l_ref, acc_ref, *, group, tq, tk):
    p = pl.program_id(2)
    i = it_ref[p]
    j = jt_ref[p]

    @pl.when(j == 0)
    def _():
        m_ref[...] = jnp.full(m_ref.shape, MASKED, F32)
        l_ref[...] = jnp.zeros(l_ref.shape, F32)
        acc_ref[...] = jnp.zeros(acc_ref.shape, F32)

    def step(masked):
        kr = kr_ref[...]
        if masked:
            adm = (kpos_ref[...] >> CHUNK_SHIFT) <= (qpos_ref[0] >> CHUNK_SHIFT)
        def scores(g):
            sl = slice(g * LANES, (g + 1) * LANES)
            q2 = jnp.concatenate([qn_ref[:, sl], qr_ref[:, sl]], axis=1)
            k2 = jnp.concatenate([kn_ref[:, sl], kr], axis=1)
            return _dot_nt(k2, q2)

        depth = 4
        ahead = [scores(g) for g in range(min(depth, group))]
        for g in range(group):
            sl = slice(g * LANES, (g + 1) * LANES)
            st = ahead.pop(0)
            if g + depth < group:
                ahead.append(scores(g + depth))
            if masked:
                st = jnp.where(adm, st, MASKED)
            m_old = m_ref[g:g + 1, :]
            m_new = jnp.maximum(m_old, jnp.max(st, axis=0, keepdims=True))
            alpha = jnp.exp2(m_old - m_new)
            pt = jnp.exp2(st - m_new)
            l_ref[g:g + 1, :] = alpha * l_ref[g:g + 1, :] + jnp.sum(pt, axis=0, keepdims=True)
            m_ref[g:g + 1, :] = m_new
            acc_ref[sl, :] = alpha * acc_ref[sl, :] + _dot_tn(v_ref[:, sl], pt.astype(BF16))

    reaches = (j + 1) * tk > i * tq

    @pl.when(reaches)
    def _():
        step(True)

    @pl.when(jnp.logical_not(reaches))
    def _():
        step(False)

    @pl.when(j == ((i + 1) * tq - 1) // tk)
    def _():
        for g in range(group):
            sl = slice(g * LANES, (g + 1) * LANES)
            o_ref[:, sl] = (acc_ref[sl, :] / l_ref[g:g + 1, :]).T.astype(o_ref.dtype)


def _mla_attention(qn, qr, kn, kr, vb, pos_row, pos_col, batch, seq, heads, tq, tk, group):
    nq = seq // tq
    nk = seq // tk
    gw = group * LANES
    pairs = [(i, j) for i in range(nq) for j in range(nk) if j * tk < (i + 1) * tq]
    it = jnp.asarray([p[0] for p in pairs], jnp.int32)
    jt = jnp.asarray([p[1] for p in pairs], jnp.int32)
    qmap = lambda b, h, p, it, jt: (b * nq + it[p], h)
    kmap = lambda b, h, p, it, jt: (b * nk + jt[p], h)
    grid_spec = pltpu.PrefetchScalarGridSpec(
        num_scalar_prefetch=2,
        grid=(batch, heads // group, len(pairs)),
        in_specs=[pl.BlockSpec((tq, gw), qmap),
                  pl.BlockSpec((tq, gw), qmap),
                  pl.BlockSpec((tk, gw), kmap),
                  pl.BlockSpec((tk, LANES), lambda b, h, p, it, jt: (b * nk + jt[p], 0)),
                  pl.BlockSpec((tk, gw), kmap),
                  pl.BlockSpec((1, 1, tq), lambda b, h, p, it, jt: (b, 0, it[p])),
                  pl.BlockSpec((tk, 1), lambda b, h, p, it, jt: (b * nk + jt[p], 0))],
        out_specs=pl.BlockSpec((tq, gw), qmap),
        scratch_shapes=[pltpu.VMEM((group, tq), F32), pltpu.VMEM((group, tq), F32),
                        pltpu.VMEM((gw, tq), F32)],
    )
    return pl.pallas_call(
        functools.partial(_mla_attn_kernel, group=group, tq=tq, tk=tk),
        grid_spec=grid_spec,
        out_shape=jax.ShapeDtypeStruct((batch * seq, heads * LANES), BF16),
        compiler_params=_params(("parallel", "parallel", "arbitrary")),
        name="mla_attn",
    )(it, jt, qn, qr, kn, kr, vb, pos_row, pos_col)


def _tree_sum8(x):
    parts = [x[r:r + 8] for r in range(0, x.shape[0], 8)]
    while len(parts) > 1:
        nxt = [parts[a] + parts[a + 1] for a in range(0, len(parts) - 1, 2)]
        if len(parts) % 2:
            nxt.append(parts[-1])
        parts = nxt
    return parts[0]


def _dsa_kernel(qi_ref, qa_ref, misc_ref, kidx_ref, ka_ref, va_ref, qpos_ref, kpos_ref,
                slopef_ref, o_ref,
                qis_ref, qas_ref, wt_ref, sc_ref, m_ref, l_ref, acc_ref,
                *, tq, kc, idx_heads, heads, topk, hg):
    i = pl.program_id(1)
    nkc = ((i + 1) * tq + kc - 1) // kc

    for h in range(idx_heads):
        qis_ref[h * tq:(h + 1) * tq, :] = qi_ref[:, h * LANES:(h + 1) * LANES]
    for h in range(heads):
        qas_ref[h * tq:(h + 1) * tq, :LANES] = qa_ref[:, h * LANES:(h + 1) * LANES]
    wt_ref[...] = misc_ref[...].T[QK_ROPE:QK_ROPE + idx_heads, :] * (
        (idx_heads ** -0.5) * (HEAD_DIM ** -0.5))
    qpos = qpos_ref[0]
    qchunk = qpos >> CHUNK_SHIFT

    def score_chunk(c, carry):
        r0 = pl.multiple_of(c * kc, kc)
        kx = kidx_ref[pl.ds(r0, kc), :].astype(BF16)
        score = jnp.zeros((kc, tq), F32)
        for g in range(idx_heads // hg):
            lt = _dot_nt(kx, qis_ref[g * hg * tq:(g + 1) * hg * tq, :])
            for hh in range(hg):
                h = g * hg + hh
                score += jnp.maximum(lt[:, hh * tq:(hh + 1) * tq], 0.0) * wt_ref[h:h + 1, :]
        adm = (kpos_ref[pl.ds(r0, kc), :] >> CHUNK_SHIFT) <= qchunk
        sc_ref[pl.ds(r0, kc), :] = jnp.where(adm, score, -jnp.inf)
        return carry

    lax.fori_loop(0, nkc, score_chunk, 0)

    def key_to_float(key):
        return pltpu.bitcast(key ^ ((key >> 31) & 0x7FFFFFFF), F32)

    def count_ge(thr):
        def body(c, cnt):
            r0 = pl.multiple_of(c * kc, kc)
            hit = jnp.where(sc_ref[pl.ds(r0, kc), :] >= thr, 1.0, 0.0)
            return cnt + _tree_sum8(hit)
        cnt8 = lax.fori_loop(0, nkc, body, jnp.zeros((8, tq), F32))
        return jnp.sum(cnt8, axis=0, keepdims=True)

    need = float(topk)
    key = jnp.where(count_ge(jnp.zeros((1, tq), F32)) >= need, 0, INT_MIN)

    def bit_step(it, key):
        cand = key + lax.shift_left(jnp.int32(1), 30 - it)
        return jnp.where(count_ge(key_to_float(cand)) >= need, cand, key)

    key = lax.fori_loop(0, 31, bit_step, key)
    key = jnp.maximum(key, NEG_INF_KEY + 1)
    thr = key_to_float(key)

    @pl.when(jnp.max(count_ge(thr)) > need)
    def _():
        keep = need - count_ge(key_to_float(key + 1))

        def ties_below(bound):
            def body(c, cnt):
                r0 = pl.multiple_of(c * kc, kc)
                idx = r0 + lax.broadcasted_iota(jnp.int32, (kc, 1), 0)
                tie = sc_ref[pl.ds(r0, kc), :] == thr
                hit = jnp.where(tie, jnp.where(idx < bound, 1.0, 0.0), 0.0)
                return cnt + _tree_sum8(hit)
            cnt8 = lax.fori_loop(0, nkc, body, jnp.zeros((8, tq), F32))
            return jnp.sum(cnt8, axis=0, keepdims=True)

        nbits = sc_ref.shape[0].bit_length()

        def bound_step(it, below):
            cand = below + lax.shift_left(jnp.int32(1), nbits - 1 - it)
            return jnp.where(ties_below(cand) < keep, cand, below)

        bound = lax.fori_loop(0, nbits, bound_step, jnp.zeros((1, tq), jnp.int32)) + 1

        def demote(c, carry):
            r0 = pl.multiple_of(c * kc, kc)
            idx = r0 + lax.broadcasted_iota(jnp.int32, (kc, 1), 0)
            sc = sc_ref[pl.ds(r0, kc), :]
            sc_ref[pl.ds(r0, kc), :] = jnp.where(
                sc == thr, jnp.where(idx >= bound, -jnp.inf, sc), sc)
            return carry

        lax.fori_loop(0, nkc, demote, 0)

    m_ref[...] = jnp.full(m_ref.shape, MASKED, F32)
    l_ref[...] = jnp.zeros(l_ref.shape, F32)
    acc_ref[...] = jnp.zeros(acc_ref.shape, F32)

    for h in range(heads):
        qas_ref[h * tq:(h + 1) * tq, LANES:] = jnp.broadcast_to(
            slopef_ref[h:h + 1, :], (tq, LANES)).astype(BF16)
    qpos0 = qpos[:, 0:1]
    lane = lax.broadcasted_iota(jnp.int32, (1, LANES), 1)

    def softmax_update(st, sel, vx, bias):
        m_all = m_ref[...]
        mask_bias = jnp.where(sel, 0.0, MASKED)
        ps, ms = [], []
        for h in range(heads):
            sl = slice(h * tq, (h + 1) * tq)
            s = st[:, sl] if bias is None else st[:, sl] - bias(h)
            s = s + mask_bias
            m_new = jnp.maximum(m_all[:, sl], jnp.max(s, axis=0, keepdims=True))
            ms.append(m_new)
            ps.append(jnp.exp2((s - m_new).astype(BF16)))
        m_new = jnp.concatenate(ms, axis=1)
        alpha = jnp.exp2(m_all - m_new)
        m_ref[...] = m_new
        vx1 = jnp.concatenate([vx, jnp.ones((vx.shape[0], SUM_COLS), BF16)], axis=1)
        pv = _dot_tn(vx1, jnp.concatenate(ps, axis=1))
        l_ref[...] = alpha * l_ref[...] + pv[LANES:LANES + 1, :]
        acc_ref[...] = alpha * acc_ref[...] + pv[:LANES, :]

    def attend_chunk(c, carry):
        r0 = pl.multiple_of(c * kc, kc)
        rel = kpos_ref[pl.ds(r0, kc), :] - qpos0
        feat = jnp.where(lane < 3, (rel >> CHUNK_SHIFT).astype(F32),
                         jnp.where(lane < 6, (rel & (CHUNK - 1)).astype(F32), 0.0))
        kx = jnp.concatenate([ka_ref[pl.ds(r0, kc), :].astype(BF16), feat.astype(BF16)], axis=1)
        vx = va_ref[pl.ds(r0, kc), :].astype(BF16)
        sel = sc_ref[pl.ds(r0, kc), :] >= jnp.where(rel < 0, thr, jnp.inf)
        softmax_update(_dot_nt(kx, qas_ref[...]), sel, vx, None)
        return carry

    lax.fori_loop(0, (i * tq + kc - 1) // kc, attend_chunk, 0)

    r0 = pl.multiple_of(i * tq, tq)
    kp = kpos_ref[pl.ds(r0, tq), :]
    shifted = (jnp.abs(kp - qpos) - (qpos - qpos0)).astype(F32)
    sel = sc_ref[pl.ds(r0, tq), :] >= thr
    st = _dot_nt(ka_ref[pl.ds(r0, tq), :].astype(BF16), qas_ref[:, :LANES])
    softmax_update(st, sel, va_ref[pl.ds(r0, tq), :].astype(BF16),
                   lambda h: (LOG2E * 2.0 ** (-8.0 * (h + 1) / heads)) * shifted)

    for h in range(heads):
        sl = slice(h * tq, (h + 1) * tq)
        o_ref[:, h * LANES:(h + 1) * LANES] = (acc_ref[:, sl] / l_ref[:, sl]).T.astype(o_ref.dtype)


def _alibi_query_features(heads):
    rows = np.zeros((heads, LANES), np.float32)
    for h in range(heads):
        rest = LOG2E * 2.0 ** (-8.0 * (h + 1) / heads)
        for k in range(3):
            piece = float(np.asarray(rest, np.float32).astype(BF16).astype(np.float32))
            rows[h, k] = CHUNK * piece
            rows[h, 3 + k] = piece
            rest -= piece
    return jnp.asarray(rows)


def _dsa(p1, p2, pos_row, pos_col, batch, seq, heads, idx_heads, topk, kv_off, kc):
    tq = LANES
    nq = seq // tq
    hw = heads * LANES
    iw = idx_heads * LANES
    assert iw % hw == 0
    kblk = kv_off // LANES
    misc_blk = p2.shape[1] // LANES - 1
    hg = min(8, idx_heads)
    kernel = functools.partial(_dsa_kernel, tq=tq, kc=kc, idx_heads=idx_heads, heads=heads,
                               topk=topk, hg=hg)
    return pl.pallas_call(
        kernel,
        grid=(batch, nq),
        in_specs=[pl.BlockSpec((tq, iw), lambda b, i: (b * nq + i, 0)),
                  pl.BlockSpec((tq, hw), lambda b, i: (b * nq + i, iw // hw)),
                  pl.BlockSpec((tq, LANES), lambda b, i: (b * nq + i, misc_blk)),
                  pl.BlockSpec((seq, LANES), lambda b, i: (b, kblk + 2)),
                  pl.BlockSpec((seq, LANES), lambda b, i: (b, kblk)),
                  pl.BlockSpec((seq, LANES), lambda b, i: (b, kblk + 1)),
                  pl.BlockSpec((1, 1, tq), lambda b, i: (b, 0, i)),
                  pl.BlockSpec((seq, 1), lambda b, i: (b, 0)),
                  pl.BlockSpec((heads, LANES), lambda b, i: (0, 0))],
        out_specs=pl.BlockSpec((tq, hw), lambda b, i: (b * nq + i, 0)),
        out_shape=jax.ShapeDtypeStruct((batch * seq, hw), BF16),
        scratch_shapes=[pltpu.VMEM((idx_heads * tq, LANES), BF16),
                        pltpu.VMEM((heads * tq, 2 * LANES), BF16),
                        pltpu.VMEM((idx_heads, tq), F32),
                        pltpu.VMEM((seq, tq), F32),
                        pltpu.VMEM((1, heads * tq), F32),
                        pltpu.VMEM((1, heads * tq), F32),
                        pltpu.VMEM((LANES, heads * tq), F32)],
        compiler_params=_params(("parallel", "arbitrary")),
        name="dsa",
    )(p1, p1, p2, p2, p2, p2, pos_row, pos_col, _alibi_query_features(heads))


def _tile(n, want):
    t = min(n, want)
    assert n % t == 0, (n, want)
    return t


def _block(x, c, positions, w_ada, b_ada, ln1_g, w_in, q_norm_g, kv_norm_g, w_uq, w_uk, w_uv,
           w_o, ln2_g, w_mlp_in, w_mlp_out, final_g, *, heads, idx_heads, b_heads):
    batch, seq, d = x.shape
    t = batch * seq
    q_lora = w_uq.shape[0]
    kv_lora = w_uk.shape[0]
    d_ff = w_mlp_in.shape[1]
    hw = heads * HEAD_DIM
    iw = idx_heads * HEAD_DIM
    bw = b_heads * HEAD_DIM
    topk = min(TOPK_MAX, seq // 4)
    assert (1 << CHUNK_SHIFT) == CHUNK and QK_ROPE + idx_heads <= LANES

    splits = (hw, HEAD_DIM, HEAD_DIM, iw, HEAD_DIM, idx_heads, q_lora, kv_lora, QK_ROPE)
    offs = [0]
    for s in splits:
        offs.append(offs[-1] + s)
    piece = lambda k, scale=1.0: (offs[k], splits[k], scale)
    pieces = (piece(3), piece(0, HEAD_DIM ** -0.5 * LOG2E), piece(6), piece(7), piece(1),
              piece(2), piece(4), piece(8), piece(5))
    packed_width = -(-offs[-1] // LANES) * LANES
    w_pack = _pack_weights(w_in.T, pieces, packed_width, _tile(d, 256))
    width1 = hw + iw
    kv_off = q_lora + kv_lora

    wq = w_uq.reshape(q_lora, b_heads, HEAD_DIM + QK_ROPE) * ((HEAD_DIM + QK_ROPE) ** -0.5 * LOG2E)
    wqn = wq[:, :, :HEAD_DIM].reshape(q_lora, bw).astype(BF16)
    wqr = jnp.pad(wq[:, :, HEAD_DIM:], ((0, 0), (0, 0), (0, LANES - QK_ROPE))
                  ).reshape(q_lora, b_heads * LANES).astype(BF16)
    wuk = w_uk.astype(BF16)
    wuv = w_uv.astype(BF16)
    wo = w_o.astype(BF16)
    w1 = w_mlp_in.astype(BF16)
    w2 = w_mlp_out.astype(BF16)

    lane = jnp.arange(LANES)
    inv_row = (ROPE_THETA ** (-(2.0 * (lane % (QK_ROPE // 2))).astype(F32) / QK_ROPE)
               ).reshape(1, LANES)
    pos_row = positions.reshape(batch, 1, seq)
    pos_col = positions.reshape(t, 1)
    x2d = x.reshape(t, d)

    mod = _adaln_mod(c, w_ada, b_ada, _tile(6 * d, 512))
    shift1, scale1, gate1, shift2, scale2, gate2 = jnp.split(mod, 6, axis=-1)

    tm_ln = _tile(seq, 512)
    h = _ln_mod(x2d, ln1_g, scale1, shift1, seq, tm_ln)
    p1, p2 = _inproj(h, w_pack, width1, _tile(t, 1024), _tile(math.gcd(width1, w_pack.shape[1] - width1), 1024))

    out_a = _dsa(p1, p2, pos_row, pos_col, batch, seq, heads, idx_heads, topk, kv_off,
                 _tile(seq, 512))

    qn, qr, kn, vb, kr = _mla_proj(p2, pos_col, inv_row, q_norm_g.reshape(1, q_lora),
                                   kv_norm_g.reshape(1, kv_lora), wqn, wqr, wuk, wuv,
                                   q_lora, kv_lora, b_heads, _tile(t, 512))
    out_b = _mla_attention(qn, qr, kn, kr, vb, pos_row, pos_col, batch, seq, b_heads,
                           _tile(seq, 512), _tile(seq, 512), min(b_heads, 16))

    tm = _tile(seq, 1024)
    x1 = _mix_resid(out_a, out_b, wo, x2d, gate1, seq, tm, _tile(d, 1024))
    h2 = _ln_mod(x1, ln2_g, scale2, shift2, seq, tm_ln)
    hid = _mm_relu2(h2, w1, tm, _tile(d_ff, 1024))
    out = _mm_resid_norm(hid, w2, x1, gate2, final_g, seq, _tile(seq, 512), _tile(d_ff, 1024),
                         "mlp_out")
    return out.reshape(batch, seq, d)


def kernel(x, c, positions, w_ada, b_ada, ln1_g, w_in, q_norm_g, kv_norm_g, w_uq, w_uk, w_uv,
           w_o, ln2_g, w_mlp_in, w_mlp_out, final_g):
    assert w_ada.shape[0] == 1, "single-layer block"
    return _block(x, c, positions, w_ada[0], b_ada[0], ln1_g[0], w_in[0], q_norm_g[0],
                  kv_norm_g[0], w_uq[0], w_uk[0], w_uv[0], w_o[0], ln2_g[0], w_mlp_in[0],
                  w_mlp_out[0], final_g, heads=A_HEADS, idx_heads=IDX_HEADS, b_heads=B_HEADS)
```

```python
import functools
import math

import jax
import jax.numpy as jnp
import numpy as np
from jax import lax
from jax.experimental import pallas as pl
from jax.experimental.pallas import tpu as pltpu

F32 = jnp.float32
BF16 = jnp.bfloat16

EPS = 1e-6
CHUNK = 64
CHUNK_SHIFT = 6
A_HEADS = 16
IDX_HEADS = 32
HEAD_DIM = 128
TOPK_MAX = 256
B_HEADS = 16
QK_ROPE = 64
ROPE_THETA = 10000.0

LANES = 128
SUM_COLS = 16
LOG2E = math.log2(math.e)
MASKED = -1e30
INT_MIN = -(2 ** 31)
NEG_INF_KEY = -2139095041
VMEM_LIMIT = 56 * 1024 * 1024
VMEM_LIMIT_WIDE = 60 * 1024 * 1024


def _params(semantics, vmem=VMEM_LIMIT):
    return pltpu.CompilerParams(dimension_semantics=semantics, vmem_limit_bytes=vmem)


def _dot_nt(a, b):
    return lax.dot_general(a, b, (((1,), (1,)), ((), ())), preferred_element_type=F32)


def _dot_tn(a, b):
    return lax.dot_general(a, b, (((0,), (0,)), ((), ())), preferred_element_type=F32)


def _mod_kernel(c_ref, w_ref, b_ref, o_ref):
    c = c_ref[...]
    s = c * (1.0 / (1.0 + jnp.exp(-c)))
    o_ref[...] = jnp.dot(s.astype(BF16), w_ref[...].astype(BF16),
                         preferred_element_type=F32) + b_ref[...]


def _adaln_mod(c, w_ada, b_ada, tn):
    b, d = c.shape
    n = w_ada.shape[1]
    rows = 8
    c_pad = jnp.zeros((rows, d), F32).at[:b].set(c)
    out = pl.pallas_call(
        _mod_kernel,
        grid=(n // tn,),
        in_specs=[pl.BlockSpec((rows, d), lambda j: (0, 0)),
                  pl.BlockSpec((d, tn), lambda j: (0, j)),
                  pl.BlockSpec((1, tn), lambda j: (0, j))],
        out_specs=pl.BlockSpec((rows, tn), lambda j: (0, j)),
        out_shape=jax.ShapeDtypeStruct((rows, n), F32),
        compiler_params=_params(("arbitrary",)),
        name="adaln_mod",
    )(c_pad, w_ada, b_ada.reshape(1, n))
    return out[:b]


def _ln_mod_kernel(x_ref, g_ref, sc_ref, sh_ref, o_ref):
    x = x_ref[...]
    y = x * lax.rsqrt(jnp.mean(x * x, axis=-1, keepdims=True) + EPS) * g_ref[...]
    o_ref[...] = (y * (1.0 + sc_ref[0]) + sh_ref[0]).astype(o_ref.dtype)


def _ln_mod(x2d, g, scale, shift, seq, tm):
    t, d = x2d.shape
    nb = scale.shape[0]
    per = seq // tm
    return pl.pallas_call(
        _ln_mod_kernel,
        grid=(t // tm,),
        in_specs=[pl.BlockSpec((tm, d), lambda i: (i, 0)),
                  pl.BlockSpec((1, d), lambda i: (0, 0)),
                  pl.BlockSpec((1, 1, d), lambda i: (i // per, 0, 0)),
                  pl.BlockSpec((1, 1, d), lambda i: (i // per, 0, 0))],
        out_specs=pl.BlockSpec((tm, d), lambda i: (i, 0)),
        out_shape=jax.ShapeDtypeStruct((t, d), BF16),
        compiler_params=_params(("parallel",)),
        name="ln_mod",
    )(x2d, g.reshape(1, d), scale.reshape(nb, 1, d), shift.reshape(nb, 1, d))


def _pack_kernel(wt_ref, o_ref, *, pieces):
    tr = o_ref.shape[0]
    sub = 4 * LANES

    def emit(rows, dst):
        o_ref[:, dst:dst + rows.shape[0]] = rows.T.astype(o_ref.dtype)

    dst = 0
    pending, pending_rows = [], 0
    for src, width, scale in pieces:
        if width % LANES == 0 and not pending:
            for off in range(0, width, sub):
                n = min(sub, width - off)
                rows = wt_ref[src + off:src + off + n, :]
                emit(rows if scale == 1.0 else rows * scale, dst + off)
            dst += width
            continue
        rows = wt_ref[src:src + width, :]
        pending.append(rows if scale == 1.0 else rows * scale)
        pending_rows += width
        assert pending_rows <= LANES
        if pending_rows == LANES:
            emit(jnp.concatenate(pending, axis=0), dst)
            dst, pending, pending_rows = dst + LANES, [], 0
    if pending:
        pending.append(jnp.zeros((LANES - pending_rows, tr), F32))
        emit(jnp.concatenate(pending, axis=0), dst)
        dst += LANES
    assert dst == o_ref.shape[1]


def _pack_weights(wt, pieces, width, tr):
    ncols, d = wt.shape
    return pl.pallas_call(
        functools.partial(_pack_kernel, pieces=pieces),
        grid=(d // tr,),
        in_specs=[pl.BlockSpec((ncols, tr), lambda i: (0, i))],
        out_specs=pl.BlockSpec((tr, width), lambda i: (i, 0)),
        out_shape=jax.ShapeDtypeStruct((d, width), BF16),
        compiler_params=_params(("parallel",)),
        name="pack_w_in",
    )(wt)


def _inproj_kernel(h_ref, w_ref, o1_ref, o2_ref, *, n1):
    j = pl.program_id(1)
    acc = lambda: jnp.dot(h_ref[...], w_ref[...], preferred_element_type=F32)

    @pl.when(j < n1)
    def _():
        o1_ref[...] = acc().astype(o1_ref.dtype)

    @pl.when(j >= n1)
    def _():
        o2_ref[...] = acc()


def _inproj(h, w, width1, tm, tn):
    t, d = h.shape
    n = w.shape[1]
    n1 = width1 // tn
    return pl.pallas_call(
        functools.partial(_inproj_kernel, n1=n1),
        grid=(t // tm, n // tn),
        in_specs=[pl.BlockSpec((tm, d), lambda i, j: (i, 0)),
                  pl.BlockSpec((d, tn), lambda i, j: (0, j))],
        out_specs=[pl.BlockSpec((tm, tn), lambda i, j: (i, jnp.minimum(j, n1 - 1))),
                   pl.BlockSpec((tm, tn), lambda i, j: (i, jnp.maximum(j - n1, 0)))],
        out_shape=[jax.ShapeDtypeStruct((t, width1), BF16),
                   jax.ShapeDtypeStruct((t, n - width1), F32)],
        compiler_params=_params(("parallel", "arbitrary")),
        name="in_proj",
    )(h, w)


def _mm_relu2_kernel(a_ref, w_ref, o_ref):
    acc = jnp.dot(a_ref[...], w_ref[...], preferred_element_type=F32)
    r = jnp.maximum(acc, 0.0)
    o_ref[...] = (r * r).astype(o_ref.dtype)


def _mm_relu2(a, w, tm, tn):
    t, d = a.shape
    n = w.shape[1]
    return pl.pallas_call(
        _mm_relu2_kernel,
        grid=(t // tm, n // tn),
        in_specs=[pl.BlockSpec((tm, d), lambda i, j: (i, 0)),
                  pl.BlockSpec((d, tn), lambda i, j: (0, j))],
        out_specs=pl.BlockSpec((tm, tn), lambda i, j: (i, j)),
        out_shape=jax.ShapeDtypeStruct((t, n), BF16),
        compiler_params=_params(("parallel", "arbitrary")),
        name="mlp_in",
    )(a, w)


def _mm_resid_kernel(a_ref, w_ref, r_ref, g_ref, fg_ref, o_ref, *, nk):
    k = pl.program_id(1)
    part = lambda: jnp.dot(a_ref[...], w_ref[...], preferred_element_type=F32)

    def finish(acc):
        y = r_ref[...] + g_ref[0] * acc
        o_ref[...] = y * lax.rsqrt(jnp.mean(y * y, axis=-1, keepdims=True) + EPS) * fg_ref[...]

    if nk == 1:
        finish(part())
        return

    @pl.when(k == 0)
    def _():
        o_ref[...] = part()

    @pl.when((k > 0) & (k < nk - 1))
    def _():
        o_ref[...] += part()

    @pl.when(k == nk - 1)
    def _():
        finish(o_ref[...] + part())


def _mm_resid_norm(a, w, resid, gate, norm_g, seq, tm, tk, name):
    t, kdim = a.shape
    n = w.shape[1]
    nb = gate.shape[0]
    per = seq // tm
    nk = kdim // tk
    return pl.pallas_call(
        functools.partial(_mm_resid_kernel, nk=nk),
        grid=(t // tm, nk),
        in_specs=[pl.BlockSpec((tm, tk), lambda i, k: (i, k)),
                  pl.BlockSpec((tk, n), lambda i, k: (k, 0)),
                  pl.BlockSpec((tm, n), lambda i, k: (i, 0)),
                  pl.BlockSpec((1, 1, n), lambda i, k: (i // per, 0, 0)),
                  pl.BlockSpec((1, n), lambda i, k: (0, 0))],
        out_specs=pl.BlockSpec((tm, n), lambda i, k: (i, 0)),
        out_shape=jax.ShapeDtypeStruct((t, n), F32),
        compiler_params=_params(("parallel", "arbitrary"), vmem=VMEM_LIMIT_WIDE),
        name=name,
    )(a, w, resid, gate.reshape(nb, 1, n), norm_g.reshape(1, n))


def _mix_resid_kernel(a1_ref, a2_ref, w1_ref, w2_ref, r_ref, g_ref, o_ref):
    acc = jnp.dot(a1_ref[...], w1_ref[...], preferred_element_type=F32)
    acc += jnp.dot(a2_ref[...], w2_ref[...], preferred_element_type=F32)
    o_ref[...] = r_ref[...] + g_ref[0] * acc


def _mix_resid(a1, a2, w, resid, gate, seq, tm, tn):
    t, k1 = a1.shape
    k2 = a2.shape[1]
    assert k1 == k2
    n = w.shape[1]
    nb = gate.shape[0]
    per = seq // tm
    return pl.pallas_call(
        _mix_resid_kernel,
        grid=(t // tm, n // tn),
        in_specs=[pl.BlockSpec((tm, k1), lambda i, j: (i, 0)),
                  pl.BlockSpec((tm, k2), lambda i, j: (i, 0)),
                  pl.BlockSpec((k1, tn), lambda i, j: (0, j)),
                  pl.BlockSpec((k2, tn), lambda i, j: (1, j)),
                  pl.BlockSpec((tm, tn), lambda i, j: (i, j)),
                  pl.BlockSpec((1, 1, tn), lambda i, j: (i // per, 0, j))],
        out_specs=pl.BlockSpec((tm, tn), lambda i, j: (i, j)),
        out_shape=jax.ShapeDtypeStruct((t, n), F32),
        compiler_params=_params(("parallel", "arbitrary")),
        name="out_proj",
    )(a1, a2, w, w, resid, gate.reshape(nb, 1, n))


def _rope_tables(pos_col, inv_row):
    ang = pos_col.astype(F32) * inv_row
    lane = lax.broadcasted_iota(jnp.int32, ang.shape, 1)
    half = QK_ROPE // 2
    cos = jnp.cos(ang)
    sin = jnp.sin(ang)
    c = jnp.where(lane < QK_ROPE, cos, 0.0)
    s_lo = jnp.where(lane < half, -sin, 0.0)
    s_hi = jnp.where((lane >= half) & (lane < QK_ROPE), sin, 0.0)
    return c, s_lo, s_hi


def _rope_apply(x, c, s_lo, s_hi):
    half = QK_ROPE // 2
    return (x * c + pltpu.roll(x, LANES - half, axis=1) * s_lo
            + pltpu.roll(x, half, axis=1) * s_hi)


def _mla_proj_kernel(cq_ref, ckv_ref, misc_ref, pos_ref, inv_ref, gq_ref, gkv_ref,
                     wqn_ref, wqr_ref, wuk_ref, wuv_ref,
                     qn_ref, qr_ref, kn_ref, vb_ref, kr_ref, *, heads):
    cq = cq_ref[...]
    cqn = (cq * lax.rsqrt(jnp.mean(cq * cq, axis=-1, keepdims=True) + EPS)
           * gq_ref[...]).astype(BF16)
    ckv = ckv_ref[...]
    kvn = (ckv * lax.rsqrt(jnp.mean(ckv * ckv, axis=-1, keepdims=True) + EPS)
           * gkv_ref[...]).astype(BF16)
    c, s_lo, s_hi = _rope_tables(pos_ref[...], inv_ref[...])

    qn_ref[...] = jnp.dot(cqn, wqn_ref[...], preferred_element_type=F32).astype(BF16)
    qr = jnp.dot(cqn, wqr_ref[...], preferred_element_type=F32)
    for h in range(heads):
        sl = slice(h * LANES, (h + 1) * LANES)
        qr_ref[:, sl] = _rope_apply(qr[:, sl], c, s_lo, s_hi).astype(BF16)
    kn_ref[...] = jnp.dot(kvn, wuk_ref[...], preferred_element_type=F32).astype(BF16)
    vb_ref[...] = jnp.dot(kvn, wuv_ref[...], preferred_element_type=F32).astype(BF16)
    kr_ref[...] = _rope_apply(misc_ref[...], c, s_lo, s_hi).astype(BF16)


def _mla_proj(p2, pos_col, inv_row, gq, gkv, wqn, wqr, wuk, wuv, q_lora, kv_lora, heads, tm):
    t = p2.shape[0]
    hw = heads * LANES
    misc_blk = p2.shape[1] // LANES - 1
    const = lambda i: (0, 0)
    return pl.pallas_call(
        functools.partial(_mla_proj_kernel, heads=heads),
        grid=(t // tm,),
        in_specs=[pl.BlockSpec((tm, q_lora), lambda i: (i, 0)),
                  pl.BlockSpec((tm, kv_lora), lambda i: (i, q_lora // kv_lora)),
                  pl.BlockSpec((tm, LANES), lambda i: (i, misc_blk)),
                  pl.BlockSpec((tm, 1), lambda i: (i, 0)),
                  pl.BlockSpec((1, LANES), const),
                  pl.BlockSpec((1, q_lora), const),
                  pl.BlockSpec((1, kv_lora), const),
                  pl.BlockSpec((q_lora, hw), const),
                  pl.BlockSpec((q_lora, hw), const),
                  pl.BlockSpec((kv_lora, hw), const),
                  pl.BlockSpec((kv_lora, hw), const)],
        out_specs=[pl.BlockSpec((tm, hw), lambda i: (i, 0))] * 4
                  + [pl.BlockSpec((tm, LANES), lambda i: (i, 0))],
        out_shape=[jax.ShapeDtypeStruct((t, hw), BF16)] * 4
                  + [jax.ShapeDtypeStruct((t, LANES), BF16)],
        compiler_params=_params(("parallel",)),
        name="mla_proj",
    )(p2, p2, p2, pos_col, inv_row, gq, gkv, wqn, wqr, wuk, wuv)


def _mla_attn_kernel(it_ref, jt_ref, qn_ref, qr_ref, kn_ref, kr_ref, v_ref, qpos_ref, kpos_ref,
                     o_ref, m_ref, l_ref, acc_ref, *, group, tq, tk):
    p = pl.program_id(2)
    i = it_ref[p]
    j = jt_ref[p]

    @pl.when(j == 0)
    def _():
        m_ref[...] = jnp.full(m_ref.shape, MASKED, F32)
        l_ref[...] = jnp.zeros(l_ref.shape, F32)
        acc_ref[...] = jnp.zeros(acc_ref.shape, F32)

    def step(masked):
        kr = kr_ref[...]
        if masked:
            adm = (kpos_ref[...] >> CHUNK_SHIFT) <= (qpos_ref[0] >> CHUNK_SHIFT)
        def scores(g):
            sl = slice(g * LANES, (g + 1) * LANES)
            q2 = jnp.concatenate([qn_ref[:, sl], qr_ref[:, sl]], axis=1)
            k2 = jnp.concatenate([kn_ref[:, sl], kr], axis=1)
            return _dot_nt(k2, q2)

        depth = 4
        ahead = [scores(g) for g in range(min(depth, group))]
        for g in range(group):
            sl = slice(g * LANES, (g + 1) * LANES)
            st = ahead.pop(0)
            if g + depth < group:
                ahead.append(scores(g + depth))
            if masked:
                st = jnp.where(adm, st, MASKED)
            m_old = m_ref[g:g + 1, :]
            m_new = jnp.maximum(m_old, jnp.max(st, axis=0, keepdims=True))
            alpha = jnp.exp2(m_old - m_new)
            m_ref[g:g + 1, :] = m_new
            if masked:
                pt = jnp.exp2((st - m_new).astype(BF16))
                v1 = jnp.concatenate([v_ref[:, sl], jnp.ones((tk, SUM_COLS), BF16)], axis=1)
                pv = _dot_tn(v1, pt)
                l_ref[g:g + 1, :] = alpha * l_ref[g:g + 1, :] + pv[LANES:LANES + 1, :]
                acc_ref[sl, :] = alpha * acc_ref[sl, :] + pv[:LANES, :]
            else:
                pt = jnp.exp2(st - m_new)
                l_ref[g:g + 1, :] = alpha * l_ref[g:g + 1, :] + jnp.sum(pt, axis=0, keepdims=True)
                acc_ref[sl, :] = alpha * acc_ref[sl, :] + _dot_tn(v_ref[:, sl], pt.astype(BF16))

    reaches = (j + 1) * tk > i * tq

    @pl.when(reaches)
    def _():
        step(True)

    @pl.when(jnp.logical_not(reaches))
    def _():
        step(False)

    @pl.when(j == ((i + 1) * tq - 1) // tk)
    def _():
        for g in range(group):
            sl = slice(g * LANES, (g + 1) * LANES)
            o_ref[:, sl] = (acc_ref[sl, :] / l_ref[g:g + 1, :]).T.astype(o_ref.dtype)


def _mla_attention(qn, qr, kn, kr, vb, pos_row, pos_col, batch, seq, heads, tq, tk, group):
    nq = seq // tq
    nk = seq // tk
    gw = group * LANES
    pairs = [(i, j) for i in range(nq) for j in range(nk) if j * tk < (i + 1) * tq]
    it = jnp.asarray([p[0] for p in pairs], jnp.int32)
    jt = jnp.asarray([p[1] for p in pairs], jnp.int32)
    qmap = lambda b, h, p, it, jt: (b * nq + it[p], h)
    kmap = lambda b, h, p, it, jt: (b * nk + jt[p], h)
    grid_spec = pltpu.PrefetchScalarGridSpec(
        num_scalar_prefetch=2,
        grid=(batch, heads // group, len(pairs)),
        in_specs=[pl.BlockSpec((tq, gw), qmap),
                  pl.BlockSpec((tq, gw), qmap),
                  pl.BlockSpec((tk, gw), kmap),
                  pl.BlockSpec((tk, LANES), lambda b, h, p, it, jt: (b * nk + jt[p], 0)),
                  pl.BlockSpec((tk, gw), kmap),
                  pl.BlockSpec((1, 1, tq), lambda b, h, p, it, jt: (b, 0, it[p])),
                  pl.BlockSpec((tk, 1), lambda b, h, p, it, jt: (b * nk + jt[p], 0))],
        out_specs=pl.BlockSpec((tq, gw), qmap),
        scratch_shapes=[pltpu.VMEM((group, tq), F32), pltpu.VMEM((group, tq), F32),
                        pltpu.VMEM((gw, tq), F32)],
    )
    return pl.pallas_call(
        functools.partial(_mla_attn_kernel, group=group, tq=tq, tk=tk),
        grid_spec=grid_spec,
        out_shape=jax.ShapeDtypeStruct((batch * seq, heads * LANES), BF16),
        compiler_params=_params(("parallel", "parallel", "arbitrary")),
        name="mla_attn",
    )(it, jt, qn, qr, kn, kr, vb, pos_row, pos_col)


def _tree_sum8(x):
    parts = [x[r:r + 8] for r in range(0, x.shape[0], 8)]
    while len(parts) > 1:
        nxt = [parts[a] + parts[a + 1] for a in range(0, len(parts) - 1, 2)]
        if len(parts) % 2:
            nxt.append(parts[-1])
        parts = nxt
    return parts[0]


def _dsa_kernel(qi_ref, qa_ref, misc_ref, kidx_ref, ka_ref, va_ref, qpos_ref, kpos_ref,
                slopef_ref, o_ref,
                qis_ref, qas_ref, wt_ref, sc_ref, m_ref, l_ref, acc_ref,
                *, tq, kc, idx_heads, heads, topk, hg):
    i = pl.program_id(1)
    nkc = ((i + 1) * tq + kc - 1) // kc

    for h in range(idx_heads):
        qis_ref[h * tq:(h + 1) * tq, :] = qi_ref[:, h * LANES:(h + 1) * LANES]
    for h in range(heads):
        qas_ref[h * tq:(h + 1) * tq, :LANES] = qa_ref[:, h * LANES:(h + 1) * LANES]
    wt_ref[...] = misc_ref[...].T[QK_ROPE:QK_ROPE + idx_heads, :] * (
        (idx_heads ** -0.5) * (HEAD_DIM ** -0.5))
    qpos = qpos_ref[0]
    qchunk = qpos >> CHUNK_SHIFT

    def score_chunk(c, carry):
        r0 = pl.multiple_of(c * kc, kc)
        kx = kidx_ref[pl.ds(r0, kc), :].astype(BF16)
        score = jnp.zeros((kc, tq), F32)
        for g in range(idx_heads // hg):
            lt = _dot_nt(kx, qis_ref[g * hg * tq:(g + 1) * hg * tq, :])
            for hh in range(hg):
                h = g * hg + hh
                score += jnp.maximum(lt[:, hh * tq:(hh + 1) * tq], 0.0) * wt_ref[h:h + 1, :]
        adm = (kpos_ref[pl.ds(r0, kc), :] >> CHUNK_SHIFT) <= qchunk
        sc_ref[pl.ds(r0, kc), :] = jnp.where(adm, score, -jnp.inf)
        return carry

    lax.fori_loop(0, nkc, score_chunk, 0)

    def key_to_float(key):
        return pltpu.bitcast(key ^ ((key >> 31) & 0x7FFFFFFF), F32)

    def count_ge(thr):
        def body(c, cnt):
            r0 = pl.multiple_of(c * kc, kc)
            hit = jnp.where(sc_ref[pl.ds(r0, kc), :] >= thr, 1.0, 0.0)
            return cnt + _tree_sum8(hit)
        cnt8 = lax.fori_loop(0, nkc, body, jnp.zeros((8, tq), F32))
        return jnp.sum(cnt8, axis=0, keepdims=True)

    need = float(topk)
    key = jnp.where(count_ge(jnp.zeros((1, tq), F32)) >= need, 0, INT_MIN)

    def bit_step(it, key):
        cand = key + lax.shift_left(jnp.int32(1), 30 - it)
        return jnp.where(count_ge(key_to_float(cand)) >= need, cand, key)

    key = lax.fori_loop(0, 31, bit_step, key)
    key = jnp.maximum(key, NEG_INF_KEY + 1)
    thr = key_to_float(key)

    @pl.when(jnp.max(count_ge(thr)) > need)
    def _():
        keep = need - count_ge(key_to_float(key + 1))

        def ties_below(bound):
            def body(c, cnt):
                r0 = pl.multiple_of(c * kc, kc)
                idx = r0 + lax.broadcasted_iota(jnp.int32, (kc, 1), 0)
                tie = sc_ref[pl.ds(r0, kc), :] == thr
                hit = jnp.where(tie, jnp.where(idx < bound, 1.0, 0.0), 0.0)
                return cnt + _tree_sum8(hit)
            cnt8 = lax.fori_loop(0, nkc, body, jnp.zeros((8, tq), F32))
            return jnp.sum(cnt8, axis=0, keepdims=True)

        nbits = sc_ref.shape[0].bit_length()

        def bound_step(it, below):
            cand = below + lax.shift_left(jnp.int32(1), nbits - 1 - it)
            return jnp.where(ties_below(cand) < keep, cand, below)

        bound = lax.fori_loop(0, nbits, bound_step, jnp.zeros((1, tq), jnp.int32)) + 1

        def demote(c, carry):
            r0 = pl.multiple_of(c * kc, kc)
            idx = r0 + lax.broadcasted_iota(jnp.int32, (kc, 1), 0)
            sc = sc_ref[pl.ds(r0, kc), :]
            sc_ref[pl.ds(r0, kc), :] = jnp.where(
                sc == thr, jnp.where(idx >= bound, -jnp.inf, sc), sc)
            return carry

        lax.fori_loop(0, nkc, demote, 0)

    m_ref[...] = jnp.full(m_ref.shape, MASKED, F32)
    l_ref[...] = jnp.zeros(l_ref.shape, F32)
    acc_ref[...] = jnp.zeros(acc_ref.shape, F32)

    for h in range(heads):
        qas_ref[h * tq:(h + 1) * tq, LANES:] = jnp.broadcast_to(
            slopef_ref[h:h + 1, :], (tq, LANES)).astype(BF16)
    qpos0 = qpos[:, 0:1]
    lane = lax.broadcasted_iota(jnp.int32, (1, LANES), 1)

    def softmax_update(st, sel, vx, bias):
        m_all = m_ref[...]
        mask_bias = jnp.where(sel, 0.0, MASKED)
        ps, ms = [], []
        for h in range(heads):
            sl = slice(h * tq, (h + 1) * tq)
            s = st[:, sl] if bias is None else st[:, sl] - bias(h)
            s = s + mask_bias
            m_new = jnp.maximum(m_all[:, sl], jnp.max(s, axis=0, keepdims=True))
            ms.append(m_new)
            ps.append(jnp.exp2((s - m_new).astype(BF16)))
        m_new = jnp.concatenate(ms, axis=1)
        alpha = jnp.exp2(m_all - m_new)
        m_ref[...] = m_new
        vx1 = jnp.concatenate([vx, jnp.ones((vx.shape[0], SUM_COLS), BF16)], axis=1)
        pv = _dot_tn(vx1, jnp.concatenate(ps, axis=1))
        l_ref[...] = alpha * l_ref[...] + pv[LANES:LANES + 1, :]
        acc_ref[...] = alpha * acc_ref[...] + pv[:LANES, :]

    def attend_chunk(c, carry):
        r0 = pl.multiple_of(c * kc, kc)
        rel = kpos_ref[pl.ds(r0, kc), :] - qpos0
        feat = jnp.where(lane < 3, (rel >> CHUNK_SHIFT).astype(F32),
                         jnp.where(lane < 6, (rel & (CHUNK - 1)).astype(F32), 0.0))
        kx = jnp.concatenate([ka_ref[pl.ds(r0, kc), :].astype(BF16), feat.astype(BF16)], axis=1)
        vx = va_ref[pl.ds(r0, kc), :].astype(BF16)
        sel = sc_ref[pl.ds(r0, kc), :] >= jnp.where(rel < 0, thr, jnp.inf)
        softmax_update(_dot_nt(kx, qas_ref[...]), sel, vx, None)
        return carry

    lax.fori_loop(0, (i * tq + kc - 1) // kc, attend_chunk, 0)

    r0 = pl.multiple_of(i * tq, tq)
    kp = kpos_ref[pl.ds(r0, tq), :]
    shifted = (jnp.abs(kp - qpos) - (qpos - qpos0)).astype(F32)
    sel = sc_ref[pl.ds(r0, tq), :] >= thr
    st = _dot_nt(ka_ref[pl.ds(r0, tq), :].astype(BF16), qas_ref[:, :LANES])
    softmax_update(st, sel, va_ref[pl.ds(r0, tq), :].astype(BF16),
                   lambda h: (LOG2E * 2.0 ** (-8.0 * (h + 1) / heads)) * shifted)

    for h in range(heads):
        sl = slice(h * tq, (h + 1) * tq)
        o_ref[:, h * LANES:(h + 1) * LANES] = (acc_ref[:, sl] / l_ref[:, sl]).T.astype(o_ref.dtype)


def _alibi_query_features(heads):
    rows = np.zeros((heads, LANES), np.float32)
    for h in range(heads):
        rest = LOG2E * 2.0 ** (-8.0 * (h + 1) / heads)
        for k in range(3):
            piece = float(np.asarray(rest, np.float32).astype(BF16).astype(np.float32))
            rows[h, k] = CHUNK * piece
            rows[h, 3 + k] = piece
            rest -= piece
    return jnp.asarray(rows)


def _dsa(p1, p2, pos_row, pos_col, batch, seq, heads, idx_heads, topk, kv_off, kc):
    tq = LANES
    nq = seq // tq
    hw = heads * LANES
    iw = idx_heads * LANES
    assert iw % hw == 0
    kblk = kv_off // LANES
    misc_blk = p2.shape[1] // LANES - 1
    hg = min(8, idx_heads)
    kernel = functools.partial(_dsa_kernel, tq=tq, kc=kc, idx_heads=idx_heads, heads=heads,
                               topk=topk, hg=hg)
    return pl.pallas_call(
        kernel,
        grid=(batch, nq),
        in_specs=[pl.BlockSpec((tq, iw), lambda b, i: (b * nq + i, 0)),
                  pl.BlockSpec((tq, hw), lambda b, i: (b * nq + i, iw // hw)),
                  pl.BlockSpec((tq, LANES), lambda b, i: (b * nq + i, misc_blk)),
                  pl.BlockSpec((seq, LANES), lambda b, i: (b, kblk + 2)),
                  pl.BlockSpec((seq, LANES), lambda b, i: (b, kblk)),
                  pl.BlockSpec((seq, LANES), lambda b, i: (b, kblk + 1)),
                  pl.BlockSpec((1, 1, tq), lambda b, i: (b, 0, i)),
                  pl.BlockSpec((seq, 1), lambda b, i: (b, 0)),
                  pl.BlockSpec((heads, LANES), lambda b, i: (0, 0))],
        out_specs=pl.BlockSpec((tq, hw), lambda b, i: (b * nq + i, 0)),
        out_shape=jax.ShapeDtypeStruct((batch * seq, hw), BF16),
        scratch_shapes=[pltpu.VMEM((idx_heads * tq, LANES), BF16),
                        pltpu.VMEM((heads * tq, 2 * LANES), BF16),
                        pltpu.VMEM((idx_heads, tq), F32),
                        pltpu.VMEM((seq, tq), F32),
                        pltpu.VMEM((1, heads * tq), F32),
                        pltpu.VMEM((1, heads * tq), F32),
                        pltpu.VMEM((LANES, heads * tq), F32)],
        compiler_params=_params(("parallel", "arbitrary")),
        name="dsa",
    )(p1, p1, p2, p2, p2, p2, pos_row, pos_col, _alibi_query_features(heads))


def _tile(n, want):
    t = min(n, want)
    assert n % t == 0, (n, want)
    return t


def _block(x, c, positions, w_ada, b_ada, ln1_g, w_in, q_norm_g, kv_norm_g, w_uq, w_uk, w_uv,
           w_o, ln2_g, w_mlp_in, w_mlp_out, final_g, *, heads, idx_heads, b_heads):
    batch, seq, d = x.shape
    t = batch * seq
    q_lora = w_uq.shape[0]
    kv_lora = w_uk.shape[0]
    d_ff = w_mlp_in.shape[1]
    hw = heads * HEAD_DIM
    iw = idx_heads * HEAD_DIM
    bw = b_heads * HEAD_DIM
    topk = min(TOPK_MAX, seq // 4)
    assert (1 << CHUNK_SHIFT) == CHUNK and QK_ROPE + idx_heads <= LANES

    splits = (hw, HEAD_DIM, HEAD_DIM, iw, HEAD_DIM, idx_heads, q_lora, kv_lora, QK_ROPE)
    offs = [0]
    for s in splits:
        offs.append(offs[-1] + s)
    piece = lambda k, scale=1.0: (offs[k], splits[k], scale)
    pieces = (piece(3), piece(0, HEAD_DIM ** -0.5 * LOG2E), piece(6), piece(7), piece(1),
              piece(2), piece(4), piece(8), piece(5))
    packed_width = -(-offs[-1] // LANES) * LANES
    w_pack = _pack_weights(w_in.T, pieces, packed_width, _tile(d, 256))
    width1 = hw + iw
    kv_off = q_lora + kv_lora

    wq = w_uq.reshape(q_lora, b_heads, HEAD_DIM + QK_ROPE) * ((HEAD_DIM + QK_ROPE) ** -0.5 * LOG2E)
    wqn = wq[:, :, :HEAD_DIM].reshape(q_lora, bw).astype(BF16)
    wqr = jnp.pad(wq[:, :, HEAD_DIM:], ((0, 0), (0, 0), (0, LANES - QK_ROPE))
                  ).reshape(q_lora, b_heads * LANES).astype(BF16)
    wuk = w_uk.astype(BF16)
    wuv = w_uv.astype(BF16)
    wo = w_o.astype(BF16)
    w1 = w_mlp_in.astype(BF16)
    w2 = w_mlp_out.astype(BF16)

    lane = jnp.arange(LANES)
    inv_row = (ROPE_THETA ** (-(2.0 * (lane % (QK_ROPE // 2))).astype(F32) / QK_ROPE)
               ).reshape(1, LANES)
    pos_row = positions.reshape(batch, 1, seq)
    pos_col = positions.reshape(t, 1)
    x2d = x.reshape(t, d)

    mod = _adaln_mod(c, w_ada, b_ada, _tile(6 * d, 512))
    shift1, scale1, gate1, shift2, scale2, gate2 = jnp.split(mod, 6, axis=-1)

    tm_ln = _tile(seq, 512)
    h = _ln_mod(x2d, ln1_g, scale1, shift1, seq, tm_ln)
    p1, p2 = _inproj(h, w_pack, width1, _tile(t, 1024), _tile(math.gcd(width1, w_pack.shape[1] - width1), 1024))

    out_a = _dsa(p1, p2, pos_row, pos_col, batch, seq, heads, idx_heads, topk, kv_off,
                 _tile(seq, 512))

    qn, qr, kn, vb, kr = _mla_proj(p2, pos_col, inv_row, q_norm_g.reshape(1, q_lora),
                                   kv_norm_g.reshape(1, kv_lora), wqn, wqr, wuk, wuv,
                                   q_lora, kv_lora, b_heads, _tile(t, 512))
    out_b = _mla_attention(qn, qr, kn, kr, vb, pos_row, pos_col, batch, seq, b_heads,
                           _tile(seq, 512), _tile(seq, 512), min(b_heads, 16))

    tm = _tile(seq, 1024)
    x1 = _mix_resid(out_a, out_b, wo, x2d, gate1, seq, tm, _tile(d, 1024))
    h2 = _ln_mod(x1, ln2_g, scale2, shift2, seq, tm_ln)
    hid = _mm_relu2(h2, w1, tm, _tile(d_ff, 1024))
    out = _mm_resid_norm(hid, w2, x1, gate2, final_g, seq, _tile(seq, 512), _tile(d_ff, 1024),
                         "mlp_out")
    return out.reshape(batch, seq, d)


def kernel(x, c, positions, w_ada, b_ada, ln1_g, w_in, q_norm_g, kv_norm_g, w_uq, w_uk, w_uv,
           w_o, ln2_g, w_mlp_in, w_mlp_out, final_g):
    assert w_ada.shape[0] == 1, "single-layer block"
    return _block(x, c, positions, w_ada[0], b_ada[0], ln1_g[0], w_in[0], q_norm_g[0],
                  kv_norm_g[0], w_uq[0], w_uk[0], w_uv[0], w_o[0], ln2_g[0], w_mlp_in[0],
                  w_mlp_out[0], final_g, heads=A_HEADS, idx_heads=IDX_HEADS, b_heads=B_HEADS)
```
